```python
import numpy as np
import jax
import jax.numpy as jnp
from jax import lax

D_MODEL = 2048
BATCH = 16
SEQ = 256
DEPTH = 2
DEC_BATCH = 4
DEC_SEQ = 1024
PAST_LEN = 512

GRID_W = 64
MIX_WIDTH = D_MODEL
POOL_WIDTH = MIX_WIDTH // 4
POOL_WINDOWS = (2, 4, 8, 16)
POOL_GROUPS = 4
POOL_GROUP_DIM = POOL_WIDTH // POOL_GROUPS
GLA_WIDTH = MIX_WIDTH // 4
GLA_DV = 128
GLA_HEADS = GLA_WIDTH // GLA_DV
GLA_DK = GLA_DV // 2
GLA_GATE_RANK = 16
GLA_TAU = 16.0
GLA_CHUNK = 64
ATT_WIDTH = MIX_WIDTH // 2
HEAD_DIM = 128
ATT_HEADS = ATT_WIDTH // HEAD_DIM
ATT_KV_HEADS = ATT_HEADS // 4
Q_BLOCK = 128
ROPE_THETA = 10000.0
N_GROUPS = 4
EXPERTS_PER_GROUP = 8
N_EXPERTS = N_GROUPS * EXPERTS_PER_GROUP
TOP_K_IN_GROUP = 2
D_EXPERT = 512
EPS = 1e-6

IN_SPLITS = (POOL_WIDTH, GLA_HEADS * GLA_DK, GLA_HEADS * GLA_DK, GLA_WIDTH, GLA_WIDTH,
             2 * GLA_GATE_RANK, ATT_WIDTH, ATT_KV_HEADS * HEAD_DIM, ATT_KV_HEADS * HEAD_DIM)
IN_COLS = 512 + 256 + 256 + 512 + 512 + 32 + 1024 + 256 + 256

kernel_name = 'hybrid_pool_gla_gqa_hmoe_diffusion_step'


def rms_norm(x, g):
    xf = x.astype(jnp.float32)
    y = xf * lax.rsqrt(jnp.mean(xf * xf, axis=-1, keepdims=True) + EPS)
    return (y * g.astype(jnp.float32)).astype(x.dtype)


def pool_mixer(u, w_pool, pool_scale):
    B, L, _ = u.shape
    uf = u.astype(jnp.float32).reshape(B, L, POOL_GROUPS, POOL_GROUP_DIM)
    cs = jnp.concatenate([jnp.zeros((B, 1, POOL_GROUPS, POOL_GROUP_DIM), jnp.float32),
                          jnp.cumsum(uf, axis=1)], axis=1)
    t = jnp.arange(L)
    means = []
    for gi, w in enumerate(POOL_WINDOWS):
        start = jnp.clip(t - w // 2, 0, L)
        end = jnp.clip(t + w // 2, 0, L)
        s = cs[:, end, gi] - cs[:, start, gi]
        means.append(s / (end - start).astype(jnp.float32)[None, :, None])
    pooled = jnp.stack(means, axis=2)
    y = jnp.einsum('blgc,gcd->blgd', (pooled - uf).astype(u.dtype), w_pool)
    return y.reshape(B, L, POOL_WIDTH) * pool_scale


def gla_scan(q, k, v, logg, s0):
    B, L, H, dk = q.shape
    dv = v.shape[-1]
    n = L // GLA_CHUNK
    to_chunks = lambda a: a.reshape(B, n, GLA_CHUNK, H, a.shape[-1]).transpose(1, 0, 3, 2, 4)
    mask = jnp.tril(jnp.ones((GLA_CHUNK, GLA_CHUNK), dtype=bool))

    def step(S, inp):
        qc, kc, vc, gc = inp
        b = jnp.cumsum(gc, axis=-2)
        o_inter = jnp.einsum('bhcd,bhde->bhce', qc * jnp.exp(b), S)
        diff = b[:, :, :, None, :] - b[:, :, None, :, :]
        decay = jnp.exp(jnp.where(mask[:, :, None], diff, -jnp.inf))
        A = jnp.einsum('bhid,bhjd,bhijd->bhij', qc, kc, decay)
        o_intra = jnp.einsum('bhij,bhje->bhie', A, vc)
        b_last = b[:, :, -1:, :]
        S_new = jnp.exp(b_last[:, :, 0, :])[..., None] * S + jnp.einsum(
            'bhjd,bhje->bhde', kc * jnp.exp(b_last - b), vc)
        return S_new, o_inter + o_intra

    S, o = lax.scan(step, s0, (to_chunks(q), to_chunks(k), to_chunks(v), to_chunks(logg)))
    o = o.transpose(1, 0, 3, 2, 4).reshape(B, L, H, dv)
    return o, S


def gla_mixer(q, k, v, og, lr, gla_up, gla_up_b, gla_norm, s0):
    B, L, _ = q.shape
    f32 = jnp.float32
    qh = q.astype(f32).reshape(B, L, GLA_HEADS, GLA_DK) * (GLA_DK ** -0.5)
    kh = k.astype(f32).reshape(B, L, GLA_HEADS, GLA_DK)
    vh = v.astype(f32).reshape(B, L, GLA_HEADS, GLA_DV)
    lr_f, lr_b = jnp.split(lr, 2, axis=-1)
    logg_f = (jax.nn.log_sigmoid((lr_f @ gla_up[0] + gla_up_b[0]).astype(f32)) / GLA_TAU
              ).reshape(B, L, GLA_HEADS, GLA_DK)
    logg_b = (jax.nn.log_sigmoid((lr_b @ gla_up[1] + gla_up_b[1]).astype(f32)) / GLA_TAU
              ).reshape(B, L, GLA_HEADS, GLA_DK)
    s0f = s0.astype(f32)
    o_f, s_f = gla_scan(qh, kh, vh, logg_f, s0f[:, 0])
    flip = lambda a: a[:, ::-1]
    o_b, s_b = gla_scan(flip(qh), flip(kh), flip(vh), flip(logg_b), s0f[:, 1])
    o = rms_norm(o_f + flip(o_b), gla_norm)
    out = o.reshape(B, L, GLA_WIDTH).astype(q.dtype) * jax.nn.silu(og)
    return out, jnp.stack([s_f, s_b], axis=1).astype(q.dtype)


def rope_2d(x, rows):
    half = HEAD_DIM // 2
    quarter = half // 2
    freqs = 1.0 / (ROPE_THETA ** (jnp.arange(quarter, dtype=jnp.float32) / quarter))
    row = jnp.repeat(jnp.arange(rows), GRID_W).astype(jnp.float32)
    col = jnp.tile(jnp.arange(GRID_W), rows).astype(jnp.float32)

    def rot(xa, pos):
        ang = pos[:, None] * freqs[None, :]
        cos = jnp.cos(ang)[None, :, None, :]
        sin = jnp.sin(ang)[None, :, None, :]
        a, b = xa[..., :quarter], xa[..., quarter:]
        return jnp.concatenate([a * cos - b * sin, a * sin + b * cos], axis=-1)

    xf = x.astype(jnp.float32)
    return jnp.concatenate([rot(xf[..., :half], row), rot(xf[..., half:], col)], axis=-1).astype(x.dtype)


def block_attention(q, k, v):
    B, Lq, H, hd = q.shape
    KV = k.shape[2]
    G = H // KV
    nb = Lq // Q_BLOCK
    qb = q.reshape(B, nb, Q_BLOCK, KV, G, hd).transpose(1, 0, 2, 3, 4, 5)
    scale = hd ** -0.5

    def attend(qi):
        s = jnp.einsum('bqkgd,bskd->bkgqs', qi, k).astype(jnp.float32) * scale
        p = jax.nn.softmax(s, axis=-1).astype(v.dtype)
        return jnp.einsum('bkgqs,bskd->bqkgd', p, v)

    o = lax.map(attend, qb)
    return o.transpose(1, 0, 2, 3, 4, 5).reshape(B, Lq, H * hd)


def hier_moe(h, w_gr, b_gr, w_er, b_er, w_ei, w_eo):
    B, L, D = h.shape
    T = B * L
    hs = h.reshape(T, D)
    f32 = jnp.float32
    g_logits = (hs @ w_gr + b_gr).astype(f32)
    g_prob = jax.nn.softmax(g_logits, axis=-1)
    g_idx = jnp.argmax(g_logits, axis=-1)
    g_w = jnp.take_along_axis(g_prob, g_idx[:, None], axis=-1)
    e_logits = (hs @ w_er + b_er).astype(f32).reshape(T, N_GROUPS, EXPERTS_PER_GROUP)
    e_logits = jnp.take_along_axis(e_logits, g_idx[:, None, None], axis=1)[:, 0]
    e_prob = jax.nn.softmax(e_logits, axis=-1)
    top_w, top_i = lax.top_k(e_prob, TOP_K_IN_GROUP)
    top_w = top_w / jnp.sum(top_w, axis=-1, keepdims=True) * g_w
    ids = g_idx[:, None] * EXPERTS_PER_GROUP + top_i
    gates = jnp.sum(jax.nn.one_hot(ids, N_EXPERTS, dtype=f32) * top_w[..., None], axis=1)
    hid = jnp.einsum('td,edf->tef', hs, w_ei)
    a, b = jnp.split(hid, 2, axis=-1)
    act = jax.nn.silu(a) * b * gates[..., None].astype(h.dtype)
    return jnp.einsum('tef,efd->td', act, w_eo).reshape(B, L, D)


def layer(x, mod, norm_mix, w_in, w_pool, pool_scale, gla_up, gla_up_b, gla_norm,
          q_norm, k_norm, w_out, norm_ffn, w_gr, b_gr, w_er, b_er, w_ei, w_eo,
          rows=None, ctx_k=None, ctx_v=None, ctx_state=None):
    B, L, _ = x.shape
    shift1, scale1, gate1, shift2, scale2, gate2 = jnp.split(mod, 6, axis=-1)
    h = rms_norm(x, norm_mix) * (1 + scale1) + shift1
    z = h @ w_in
    offsets = np.cumsum(IN_SPLITS)[:-1].tolist()
    u_pool, q_g, k_g, v_g, og_g, lr_g, q_a, k_a, v_a = jnp.split(z, offsets, axis=-1)

    pool_out = pool_mixer(u_pool, w_pool, pool_scale)

    if ctx_state is None:
        s0 = jnp.zeros((B, 2, GLA_HEADS, GLA_DK, GLA_DV), jnp.float32)
    else:
        s0 = ctx_state
    gla_out, gla_state = gla_mixer(q_g, k_g, v_g, og_g, lr_g, gla_up, gla_up_b, gla_norm, s0)

    qa = rms_norm(q_a.reshape(B, L, ATT_HEADS, HEAD_DIM), q_norm)
    ka = rms_norm(k_a.reshape(B, L, ATT_KV_HEADS, HEAD_DIM), k_norm)
    va = v_a.reshape(B, L, ATT_KV_HEADS, HEAD_DIM)
    if ctx_k is None:
        keys, vals = ka, va
    else:
        qa = rope_2d(qa, rows)
        keys = jnp.concatenate([ctx_k.astype(x.dtype), rope_2d(ka, rows)], axis=1)
        vals = jnp.concatenate([ctx_v.astype(x.dtype), va], axis=1)
    attn_out = block_attention(qa, keys, vals)

    mixed = jnp.concatenate([pool_out, gla_out, attn_out], axis=-1) @ w_out
    x = x + gate1 * mixed
    h2 = rms_norm(x, norm_ffn) * (1 + scale2) + shift2
    x = x + gate2 * hier_moe(h2, w_gr, b_gr, w_er, b_er, w_ei, w_eo)
    return x, ka, va, gla_state


def setup_inputs(seed: int = 0) -> dict:
    key = jax.random.key(seed)
    ks = jax.random.split(key, 26)
    f32 = jnp.float32
    nrm = lambda k, shape, s: jax.random.normal(k, shape, f32) * s
    D = D_MODEL
    return {
        'x_prompt': nrm(ks[0], (BATCH, SEQ, D), 1.0),
        'x_sample': nrm(ks[1], (DEC_BATCH, DEC_SEQ, D), 1.0),
        'cache_k': nrm(ks[2], (DEC_BATCH, DEPTH, PAST_LEN, ATT_KV_HEADS, HEAD_DIM), 1.0),
        'cache_v': nrm(ks[3], (DEC_BATCH, DEPTH, PAST_LEN, ATT_KV_HEADS, HEAD_DIM), 1.0),
        'state_gla': nrm(ks[4], (DEC_BATCH, DEPTH, 2, GLA_HEADS, GLA_DK, GLA_DV), 2.0),
        'c': nrm(ks[5], (DEC_BATCH, D), 1.0),
        'c_ctx': nrm(ks[6], (D,), 1.0),
        'w_mod': nrm(ks[7], (DEPTH, D, 6 * D), 0.5 * D ** -0.5),
        'b_mod': nrm(ks[8], (DEPTH, 6 * D), 0.02),
        'norm_mix': 1.0 + nrm(ks[9], (DEPTH, D), 0.05),
        'w_in': nrm(ks[10], (DEPTH, D, IN_COLS), D ** -0.5),
        'w_pool': nrm(ks[11], (DEPTH, POOL_GROUPS, POOL_GROUP_DIM, POOL_GROUP_DIM), POOL_GROUP_DIM ** -0.5),
        'pool_scale': 1.0 + nrm(ks[12], (DEPTH, POOL_WIDTH), 0.1),
        'gla_up': nrm(ks[13], (DEPTH, 2, GLA_GATE_RANK, GLA_HEADS * GLA_DK), GLA_GATE_RANK ** -0.5),
        'gla_up_b': nrm(ks[14], (DEPTH, 2, GLA_HEADS * GLA_DK), 0.5),
        'gla_norm': 1.0 + nrm(ks[15], (DEPTH, GLA_DV), 0.05),
        'q_norm': 1.0 + nrm(ks[16], (DEPTH, HEAD_DIM), 0.05),
        'k_norm': 1.0 + nrm(ks[17], (DEPTH, HEAD_DIM), 0.05),
        'w_out': nrm(ks[18], (DEPTH, MIX_WIDTH, D), MIX_WIDTH ** -0.5),
        'norm_ffn': 1.0 + nrm(ks[19], (DEPTH, D), 0.05),
        'w_group_router': nrm(ks[20], (DEPTH, D, N_GROUPS), D ** -0.5),
        'b_group_router': nrm(ks[21], (DEPTH, N_GROUPS), 0.01),
        'w_expert_router': nrm(ks[22], (DEPTH, D, N_EXPERTS), D ** -0.5),
        'b_expert_router': nrm(ks[23], (DEPTH, N_EXPERTS), 0.01),
        'w_expert_in': nrm(ks[24], (DEPTH, N_EXPERTS, D, 2 * D_EXPERT), D ** -0.5),
        'w_expert_out': nrm(ks[25], (DEPTH, N_EXPERTS, D_EXPERT, D), D_EXPERT ** -0.5),
    }


def reference(x_prompt, x_sample, cache_k, cache_v, state_gla, c, c_ctx, w_mod, b_mod,
              norm_mix, w_in, w_pool, pool_scale, gla_up, gla_up_b, gla_norm, q_norm, k_norm,
              w_out, norm_ffn, w_group_router, b_group_router, w_expert_router, b_expert_router,
              w_expert_in, w_expert_out):
    rows = x_sample.shape[1] // GRID_W
    xp, xs = x_prompt, x_sample
    new_k, new_v, new_s = [], [], []
    for l in range(DEPTH):
        w = (norm_mix[l], w_in[l], w_pool[l], pool_scale[l], gla_up[l], gla_up_b[l], gla_norm[l],
             q_norm[l], k_norm[l], w_out[l], norm_ffn[l], w_group_router[l], b_group_router[l],
             w_expert_router[l], b_expert_router[l], w_expert_in[l], w_expert_out[l])
        mod_ctx = (jax.nn.silu(c_ctx) @ w_mod[l] + b_mod[l])[None, None, :]
        mod_lat = (jax.nn.silu(c) @ w_mod[l] + b_mod[l])[:, None, :]
        xp, kc, vc, sc = layer(xp, mod_ctx, *w)
        new_k.append(kc)
        new_v.append(vc)
        new_s.append(sc)
        xs, _, _, _ = layer(xs, mod_lat, *w, rows=rows, ctx_k=cache_k[:, l],
                            ctx_v=cache_v[:, l], ctx_state=state_gla[:, l])
    new_cache_k = jnp.stack(new_k, axis=1)
    new_cache_v = jnp.stack(new_v, axis=1)
    new_state_gla = jnp.stack(new_s, axis=1)
    return (xp, xs, new_cache_k, new_cache_v, new_state_gla)
```

```python
import functools

import numpy as np
import jax
import jax.numpy as jnp
from jax import lax
from jax.experimental import pallas as pl
from jax.experimental.pallas import tpu as pltpu

F32 = jnp.float32
BF16 = jnp.bfloat16
HIGHEST = lax.Precision.HIGHEST

D_MODEL = 2048
BATCH = 16
SEQ = 256
DEPTH = 2
DEC_BATCH = 4
DEC_SEQ = 1024
PAST_LEN = 512
GRID_W = 64
POOL_WIDTH = 512
POOL_WINDOWS = (2, 4, 8, 16)
POOL_GROUPS = 4
POOL_GROUP_DIM = 128
GLA_WIDTH = 512
GLA_DV = 128
GLA_HEADS = 4
GLA_DK = 64
GLA_GATE_RANK = 16
GLA_TAU = 16.0
GLA_CHUNK = 64
GLA_SUB = 16
ATT_WIDTH = 1024
HEAD_DIM = 128
ATT_HEADS = 8
ATT_KV_HEADS = 2
ROPE_THETA = 10000.0
N_GROUPS = 4
EXPERTS_PER_GROUP = 8
N_EXPERTS = 32
D_EXPERT = 512
EPS = 1e-6

T_CTX = BATCH * SEQ
T_LAT = DEC_BATCH * DEC_SEQ
T_ALL = T_CTX + T_LAT
TM = 256
N_TOK_TILES = T_ALL // TM
CTX_TILES = T_CTX // TM
LAT_TILES_PER_SEQ = DEC_SEQ // TM
CTX_MOD_ROW = DEC_BATCH

Z_POOL = 0
Z_QG = 512
Z_KG = 768
Z_VG = 1024
Z_OG = 1536
Z_QA = 2048
Z_KA = 3072
Z_VA = 3328
Z_LR = 3584
Z_COLS = 3840
Z_NBLK = 3
Z_BN = Z_COLS // Z_NBLK

MOE_TILE = 256
MOE_SLOTS = T_ALL * 2
MOE_TILES = MOE_SLOTS // MOE_TILE + N_EXPERTS
ROUTE_LANES = 128
VMEM_LIMIT = 56 * 1024 * 1024


def _silu(x):
    return x / (1.0 + jnp.exp(-x))


def _rms(x, g):
    return x * lax.rsqrt(jnp.mean(x * x, axis=-1, keepdims=True) + EPS) * g


def _mod_row(i):
    return jnp.where(i < CTX_TILES, CTX_MOD_ROW, (i - CTX_TILES) // LAT_TILES_PER_SEQ)


def _mod_kernel(c_ref, w_ref, b_ref, o_ref):
    s = _silu(c_ref[...]).astype(BF16)
    o_ref[...] = jnp.dot(s, w_ref[...].astype(BF16), preferred_element_type=F32) + b_ref[...]


def _mod_call(c8, w_mod, b_mod):
    bn = 1536
    return pl.pallas_call(
        _mod_kernel,
        out_shape=jax.ShapeDtypeStruct((DEPTH, 8, 6 * D_MODEL), F32),
        grid=(DEPTH, 6 * D_MODEL // bn),
        in_specs=[
            pl.BlockSpec((8, D_MODEL), lambda l, j: (0, 0)),
            pl.BlockSpec((None, D_MODEL, bn), lambda l, j: (l, 0, j)),
            pl.BlockSpec((None, 1, bn), lambda l, j: (l, 0, j)),
        ],
        out_specs=pl.BlockSpec((None, 8, bn), lambda l, j: (l, 0, j)),
        compiler_params=pltpu.CompilerParams(
            dimension_semantics=("arbitrary", "arbitrary"), vmem_limit_bytes=VMEM_LIMIT),
        name="mod_table",
    )(c8, w_mod, b_mod.reshape(DEPTH, 1, 6 * D_MODEL))


def _inproj_kernel(x_ref, mod_ref, g_ref, w_ref, z_ref):
    y = _rms(x_ref[...], g_ref[...])
    shift = mod_ref[:, 0:D_MODEL]
    scale = mod_ref[:, D_MODEL:2 * D_MODEL]
    h = y * (1.0 + scale) + shift
    z_ref[...] = jnp.dot(h.astype(BF16), w_ref[...], preferred_element_type=F32)


def _inproj_call(x, mod3, g, w):
    return pl.pallas_call(
        _inproj_kernel,
        out_shape=jax.ShapeDtypeStruct((T_ALL, Z_COLS), F32),
        grid=(Z_NBLK, N_TOK_TILES),
        in_specs=[
            pl.BlockSpec((TM, D_MODEL), lambda j, i: (i, 0)),
            pl.BlockSpec((None, 1, 6 * D_MODEL), lambda j, i: (_mod_row(i), 0, 0)),
            pl.BlockSpec((1, D_MODEL), lambda j, i: (0, 0)),
            pl.BlockSpec((D_MODEL, Z_BN), lambda j, i: (0, j)),
        ],
        out_specs=pl.BlockSpec((TM, Z_BN), lambda j, i: (i, j)),
        compiler_params=pltpu.CompilerParams(
            dimension_semantics=("arbitrary", "arbitrary"), vmem_limit_bytes=VMEM_LIMIT),
        name="in_proj",
    )(x, mod3, g, w)


def _pool_kernel(u_ref, band_ref, invc_ref, wp_ref, ps_ref, o_ref):
    for g in range(POOL_GROUPS):
        cs = slice(g * POOL_GROUP_DIM, (g + 1) * POOL_GROUP_DIM)
        u = u_ref[:, cs]
        hi = u.astype(BF16)
        lo = (u - hi.astype(F32)).astype(BF16)
        band = band_ref[g]
        s = (jnp.dot(band, hi, preferred_element_type=F32)
             + jnp.dot(band, lo, preferred_element_type=F32))
        d = s * invc_ref[:, cs] - u
        y = jnp.dot(d.astype(BF16), wp_ref[g], preferred_element_type=F32)
        o_ref[:, cs] = (y * ps_ref[:, cs]).astype(BF16)


def _pool_consts(L):
    t = np.arange(L)
    band = np.zeros((POOL_GROUPS, L, L), np.float32)
    invc = np.zeros((L, POOL_WIDTH), np.float32)
    for gi, w in enumerate(POOL_WINDOWS):
        start = np.clip(t - w // 2, 0, L)
        end = np.clip(t + w // 2, 0, L)
        band[gi] = (t[None, :] >= start[:, None]) & (t[None, :] < end[:, None])
        invc[:, gi * POOL_GROUP_DIM:(gi + 1) * POOL_GROUP_DIM] = (1.0 / (end - start))[:, None]
    return jnp.asarray(band, BF16), jnp.asarray(invc, F32)


def _pool_call(z, wp, ps, L, nseq, blk0):
    band, invc = _pool_consts(L)
    return pl.pallas_call(
        _pool_kernel,
        out_shape=jax.ShapeDtypeStruct((nseq * L, POOL_WIDTH), BF16),
        grid=(nseq,),
        in_specs=[
            pl.BlockSpec((L, POOL_WIDTH), lambda s: (blk0 + s, Z_POOL // POOL_WIDTH)),
            pl.BlockSpec((POOL_GROUPS, L, L), lambda s: (0, 0, 0)),
            pl.BlockSpec((L, POOL_WIDTH), lambda s: (0, 0)),
            pl.BlockSpec((POOL_GROUPS, POOL_GROUP_DIM, POOL_GROUP_DIM), lambda s: (0, 0, 0)),
            pl.BlockSpec((1, POOL_WIDTH), lambda s: (0, 0)),
        ],
        out_specs=pl.BlockSpec((L, POOL_WIDTH), lambda s: (s, 0)),
        compiler_params=pltpu.CompilerParams(
            dimension_semantics=("arbitrary",), vmem_limit_bytes=VMEM_LIMIT),
        name="pool_mixer_%d" % L,
    )(z, band, invc, wp, ps)


_QK_LANES = GLA_HEADS * GLA_DK
_V_LANES = GLA_HEADS * GLA_DV
_NSUB = GLA_CHUNK // GLA_SUB
_PAIR = GLA_SUB * GLA_SUB


def _gla_consts():
    C, S = GLA_CHUNK, GLA_SUB
    r = np.arange(C)
    cum = np.stack([(r[None, :] <= r[:, None]), (r[None, :] >= r[:, None])]).astype(np.float32)
    lane_qk = np.arange(_QK_LANES)
    col_j, col_h, col_jj = lane_qk // 64, (lane_qk % 64) // S, lane_qk % S
    seg = (lane_qk[:, None] // GLA_DK == col_h[None, :]).astype(np.float32)
    pair = np.arange(_PAIR)
    pi, pj = pair // S, pair % S
    gsum = (np.arange(S)[:, None] == pi[None, :]).astype(np.float32)
    dmask = np.zeros((2, _NSUB, _PAIR, _QK_LANES), np.float32)
    for dr in range(2):
        earlier = (pj <= pi) if dr == 0 else (pj >= pi)
        for I in range(_NSUB):
            dmask[dr, I] = (earlier[:, None] & (pj[:, None] == col_jj[None, :])
                            & (col_j[None, :] == I))
    rows64 = np.arange(4 * S)
    hmask_k = (rows64[:, None] // S == lane_qk[None, :] // GLA_DK).astype(np.float32)
    lane_v = np.arange(_V_LANES)
    hmask_v = (rows64[:, None] // S == lane_v[None, :] // GLA_DV).astype(np.float32)
    bdmask = (lane_v[:, None] // GLA_DV == lane_qk[None, :] // GLA_DK).astype(np.float32)
    return (jnp.asarray(cum), jnp.asarray(seg, BF16), jnp.asarray(gsum, BF16), jnp.asarray(dmask),
            jnp.asarray(hmask_k), jnp.asarray(hmask_v), jnp.asarray(bdmask))


def _gla_kernel(*refs, L, has_state, emit_state):
    (q_ref, k_ref, v_ref, og_ref, lr_ref, up_ref, upb_ref, gn_ref, cum_ref, seg_ref, gsum_ref,
     dmask_ref, hmk_ref, hmv_ref, bdm_ref) = refs[:15]
    pos = 15
    s0_ref = None
    if has_state:
        s0_ref = refs[pos]
        pos += 1
    o_ref = refs[pos]
    pos += 1
    st_out_ref = None
    if emit_state:
        st_out_ref = refs[pos]
        pos += 1
    st_scr, oacc_scr = refs[pos], refs[pos + 1]

    nchunk = L // GLA_CHUNK
    C, S = GLA_CHUNK, GLA_SUB
    nt_dims = (((1,), (1,)), ((), ()))
    tn_dims = (((0,), (0,)), ((), ()))

    for direction in range(2):
        rev = direction == 1
        if has_state:
            st_scr[...] = s0_ref[direction]
        else:
            st_scr[...] = jnp.zeros_like(st_scr)

        def chunk_body(ci, carry, rev=rev, direction=direction):
            c = (nchunk - 1 - ci) if rev else ci
            r0 = pl.multiple_of(c * C, C)
            q = q_ref[pl.ds(r0, C), :] * (GLA_DK ** -0.5)
            k = k_ref[pl.ds(r0, C), :]
            v = v_ref[pl.ds(r0, C), :]
            x = jnp.dot(lr_ref[pl.ds(r0, C), :], up_ref[direction], precision=HIGHEST,
                        preferred_element_type=F32) + upb_ref[direction]
            g = (jnp.minimum(x, 0.0) - jnp.log1p(jnp.exp(-jnp.abs(x)))) / GLA_TAU
            b = jnp.dot(cum_ref[direction], g, precision=HIGHEST, preferred_element_type=F32)
            b_end = b[0:1] if rev else b[C - 1:C]

            st = st_scr[...]
            o = lax.dot_general((q * jnp.exp(b)).astype(BF16), st.astype(BF16), nt_dims,
                                preferred_element_type=F32)

            xs = []
            for I in range(_NSUB):
                q_i, k_i, b_i = q[I * S:(I + 1) * S], k[I * S:(I + 1) * S], b[I * S:(I + 1) * S]
                for i in range(S):
                    dec = jnp.exp(jnp.minimum(b_i[i:i + 1] - b_i, 0.0))
                    xs.append((q_i[i:i + 1] * k_i) * dec)
            xall = jnp.concatenate(xs, axis=0).astype(BF16)
            rall = jnp.dot(xall, seg_ref[...], preferred_element_type=F32)

            a_rows = []
            for I in range(_NSUB):
                rm = (rall[I * _PAIR:(I + 1) * _PAIR] * dmask_ref[direction, I]).astype(BF16)
                a_i = jnp.dot(gsum_ref[...], rm, preferred_element_type=F32)
                first = (I == _NSUB - 1) if rev else (I == 0)
                if not first:
                    ref_row = (I + 1) * S if rev else I * S - 1
                    r_i = b[ref_row:ref_row + 1]
                    qt = q[I * S:(I + 1) * S] * jnp.exp(b[I * S:(I + 1) * S] - r_i)
                    kt = k * jnp.exp(jnp.minimum(r_i - b, 0.0))
                    blocks = []
                    for J in range(_NSUB):
                        earlier = (J > I) if rev else (J < I)
                        if earlier:
                            kj = kt[J * S:(J + 1) * S]
                            blocks.append(jnp.concatenate([kj] * GLA_HEADS, axis=0) * hmk_ref[...])
                        else:
                            blocks.append(jnp.zeros((GLA_HEADS * S, _QK_LANES), F32))
                    rhs = jnp.concatenate(blocks, axis=0).astype(BF16)
                    a_i = a_i + lax.dot_general(qt.astype(BF16), rhs, nt_dims,
                                                preferred_element_type=F32)
                a_rows.append(a_i)
            a = jnp.concatenate(a_rows, axis=0).astype(BF16)

            vblocks = []
            for J in range(_NSUB):
                vj = v[J * S:(J + 1) * S]
                vblocks.append(jnp.concatenate([vj] * GLA_HEADS, axis=0) * hmv_ref[...])
            vbd = jnp.concatenate(vblocks, axis=0).astype(BF16)
            o = o + jnp.dot(a, vbd, preferred_element_type=F32)
            if rev:
                oacc_scr[pl.ds(r0, C), :] = oacc_scr[pl.ds(r0, C), :] + o
            else:
                oacc_scr[pl.ds(r0, C), :] = o

            kl = (k * jnp.exp(b_end - b)).astype(BF16)
            u = lax.dot_general(v.astype(BF16), kl, tn_dims, preferred_element_type=F32)
            st_scr[...] = st * jnp.exp(b_end) + u * bdm_ref[...]
            return carry

        lax.fori_loop(0, nchunk, chunk_body, 0)
        if emit_state:
            st_out_ref[direction] = st_scr[...]

    for h in range(GLA_HEADS):
        cs = slice(h * GLA_DV, (h + 1) * GLA_DV)
        oh = _rms(oacc_scr[:, cs], gn_ref[...])
        o_ref[:, cs] = (oh * _silu(og_ref[:, cs])).astype(BF16)


def _gla_call(z, up, upb, gn, consts, s0t, L, nseq, blk0, emit_state):
    has_state = s0t is not None
    cum, seg, gsum, dmask, hmk, hmv, bdm = consts
    full = lambda shape: pl.BlockSpec(shape, lambda s: (0,) * len(shape))
    in_specs = [
        pl.BlockSpec((L, _QK_LANES), lambda s: (blk0 + s, Z_QG // _QK_LANES)),
        pl.BlockSpec((L, _QK_LANES), lambda s: (blk0 + s, Z_KG // _QK_LANES)),
        pl.BlockSpec((L, _V_LANES), lambda s: (blk0 + s, Z_VG // _V_LANES)),
        pl.BlockSpec((L, _V_LANES), lambda s: (blk0 + s, Z_OG // _V_LANES)),
        pl.BlockSpec((L, 128), lambda s: (blk0 + s, Z_LR // 128)),
        full(up.shape), full(upb.shape), full(gn.shape), full(cum.shape), full(seg.shape),
        full(gsum.shape), full(dmask.shape), full(hmk.shape), full(hmv.shape), full(bdm.shape),
    ]
    args = [z, z, z, z, z, up, upb, gn, cum, seg, gsum, dmask, hmk, hmv, bdm]
    if has_state:
        in_specs.append(pl.BlockSpec((None, 2, _V_LANES, _QK_LANES), lambda s: (s, 0, 0, 0)))
        args.append(s0t)
    out_shape = [jax.ShapeDtypeStruct((nseq * L, GLA_WIDTH), BF16)]
    out_specs = [pl.BlockSpec((L, GLA_WIDTH), lambda s: (s, 0))]
    if emit_state:
        out_shape.append(jax.ShapeDtypeStruct((nseq, 2, _V_LANES, _QK_LANES), F32))
        out_specs.append(pl.BlockSpec((None, 2, _V_LANES, _QK_LANES), lambda s: (s, 0, 0, 0)))
    return pl.pallas_call(
        functools.partial(_gla_kernel, L=L, has_state=has_state, emit_state=emit_state),
        out_shape=out_shape,
        grid=(nseq,),
        in_specs=in_specs,
        out_specs=out_specs,
        scratch_shapes=[pltpu.VMEM((_V_LANES, _QK_LANES), F32), pltpu.VMEM((L, _V_LANES), F32)],
        compiler_params=pltpu.CompilerParams(
            dimension_semantics=("arbitrary",), vmem_limit_bytes=VMEM_LIMIT),
        name="gla_mixer_%d" % L,
    )(*args)


_TQ = 256


def _rope(x, cos, sin_signed, first_of_pair):
    partner = jnp.where(first_of_pair, pltpu.roll(x, HEAD_DIM - 32, 1), pltpu.roll(x, 32, 1))
    return x * cos + partner * sin_signed


def _attn_kernel(*refs, L, latent):
    q_ref, k_ref, v_ref, qn_ref, kn_ref = refs[:5]
    if latent:
        cos_ref, sin_ref, ck_ref, cv_ref, o_ref = refs[5:10]
    else:
        o_ref, ka_ref, va_ref = refs[5:8]
    G = ATT_HEADS // ATT_KV_HEADS
    scale = HEAD_DIM ** -0.5
    kn = _rms(k_ref[...], kn_ref[...])
    v = v_ref[...]
    if latent:
        lane = lax.broadcasted_iota(jnp.int32, (1, HEAD_DIM), 1)
        first_of_pair = (lane % 64) < 32
        kr = _rope(kn, cos_ref[...], sin_ref[...], first_of_pair)
        keys = jnp.concatenate([ck_ref[...], kr], axis=0).astype(BF16)
        vals = jnp.concatenate([cv_ref[...], v], axis=0).astype(BF16)
    else:
        ka_ref[...] = kn
        va_ref[...] = v
        keys = kn.astype(BF16)
        vals = v.astype(BF16)
    for g in range(G):
        cs = slice(g * HEAD_DIM, (g + 1) * HEAD_DIM)
        for qb in range(L // _TQ):
            rs = slice(qb * _TQ, (qb + 1) * _TQ)
            qn = _rms(q_ref[rs, cs], qn_ref[...])
            if latent:
                qn = _rope(qn, cos_ref[rs, :], sin_ref[rs, :], first_of_pair)
            s = lax.dot_general(qn.astype(BF16), keys, (((1,), (1,)), ((), ())),
                                preferred_element_type=F32) * scale
            p = jnp.exp(s - jnp.max(s, axis=-1, keepdims=True))
            p = p / jnp.sum(p, axis=-1, keepdims=True)
            o = jnp.dot(p.astype(BF16), vals, preferred_element_type=F32)
            o_ref[rs, cs] = o.astype(BF16)


def _rope_tables():
    rows = DEC_SEQ // GRID_W
    half = HEAD_DIM // 2
    quarter = half // 2
    freqs = 1.0 / (ROPE_THETA ** (jnp.arange(quarter, dtype=jnp.float32) / quarter))
    row = jnp.repeat(jnp.arange(rows), GRID_W).astype(jnp.float32)
    col = jnp.tile(jnp.arange(GRID_W), rows).astype(jnp.float32)

    def tab(p):
        ang = p[:, None] * freqs[None, :]
        c, s = jnp.cos(ang), jnp.sin(ang)
        return jnp.concatenate([c, c], axis=-1), jnp.concatenate([-s, s], axis=-1)

    cr, sr = tab(row)
    cc, sc = tab(col)
    return jnp.concatenate([cr, cc], axis=-1), jnp.concatenate([sr, sc], axis=-1)


def _attn_call(z, qn, kn, L, nseq, blk0, latent, rope=None, ck=None, cv=None, layer=0):
    G = ATT_HEADS // ATT_KV_HEADS
    qw = G * HEAD_DIM
    in_specs = [
        pl.BlockSpec((L, qw), lambda s, kv: (blk0 + s, Z_QA // qw + kv)),
        pl.BlockSpec((L, HEAD_DIM), lambda s, kv: (blk0 + s, Z_KA // HEAD_DIM + kv)),
        pl.BlockSpec((L, HEAD_DIM), lambda s, kv: (blk0 + s, Z_VA // HEAD_DIM + kv)),
        pl.BlockSpec((1, HEAD_DIM), lambda s, kv: (0, 0)),
        pl.BlockSpec((1, HEAD_DIM), lambda s, kv: (0, 0)),
    ]
    args = [z, z, z, qn, kn]
    out_shape = [jax.ShapeDtypeStruct((nseq * L, ATT_WIDTH), BF16)]
    out_specs = [pl.BlockSpec((L, qw), lambda s, kv: (s, kv))]
    if latent:
        in_specs += [
            pl.BlockSpec((L, HEAD_DIM), lambda s, kv: (0, 0)),
            pl.BlockSpec((L, HEAD_DIM), lambda s, kv: (0, 0)),
            pl.BlockSpec((None, None, PAST_LEN, HEAD_DIM), lambda s, kv: (s, layer, 0, kv)),
            pl.BlockSpec((None, None, PAST_LEN, HEAD_DIM), lambda s, kv: (s, layer, 0, kv)),
        ]
        args += [rope[0], rope[1], ck, cv]
    else:
        kv_shape = jax.ShapeDtypeStruct((nseq, L, ATT_KV_HEADS * HEAD_DIM), F32)
        out_shape += [kv_shape, kv_shape]
        kv_spec = pl.BlockSpec((None, L, HEAD_DIM), lambda s, kv: (s, 0, kv))
        out_specs += [kv_spec, kv_spec]
    return pl.pallas_call(
        functools.partial(_attn_kernel, L=L, latent=latent),
        out_shape=out_shape,
        grid=(nseq, ATT_KV_HEADS),
        in_specs=in_specs,
        out_specs=out_specs,
        compiler_params=pltpu.CompilerParams(
            dimension_semantics=("arbitrary", "arbitrary"), vmem_limit_bytes=VMEM_LIMIT),
        name="attn_mixer_%d" % L,
    )(*args)


def _route(logits):
    lane = lax.broadcasted_iota(jnp.int32, logits.shape, 1)
    lane_f = lane.astype(F32)
    neg = jnp.float32(-jnp.inf)
    big = jnp.float32(ROUTE_LANES)
    is_g = lane < N_GROUPS
    gl = jnp.where(is_g, logits, neg)
    gmax = jnp.max(gl, axis=-1, keepdims=True)
    g_idx = jnp.min(jnp.where(gl == gmax, lane_f, big), axis=-1, keepdims=True)
    g_w = 1.0 / jnp.sum(jnp.where(is_g, jnp.exp(gl - gmax), 0.0), axis=-1, keepdims=True)
    e_lane = lane - N_GROUPS
    in_group = (e_lane >= 0) & (e_lane < N_EXPERTS) & (
        jnp.right_shift(e_lane, 3).astype(F32) == g_idx)
    el = jnp.where(in_group, logits, neg)
    m1 = jnp.max(el, axis=-1, keepdims=True)
    i1 = jnp.min(jnp.where(el == m1, lane_f, big), axis=-1, keepdims=True)
    el2 = jnp.where(lane_f == i1, neg, el)
    m2 = jnp.max(el2, axis=-1, keepdims=True)
    i2 = jnp.min(jnp.where(el2 == m2, lane_f, big), axis=-1, keepdims=True)
    zsum = jnp.sum(jnp.where(in_group, jnp.exp(el - m1), 0.0), axis=-1, keepdims=True)
    p1 = 1.0 / zsum
    p2 = jnp.exp(m2 - m1) / zsum
    w1 = p1 / (p1 + p2) * g_w
    w2 = p2 / (p1 + p2) * g_w
    out = jnp.where(lane == 0, i1 - N_GROUPS, 0.0)
    out = jnp.where(lane == 1, i2 - N_GROUPS, out)
    out = jnp.where(lane == 2, w1, out)
    out = jnp.where(lane == 3, w2, out)
    return out


def _outproj_kernel(pool_ref, gla_ref, attn_ref, x_ref, mod_ref, wo_ref, nf_ref, wr_ref, br_ref,
                    x1_ref, h2_ref, route_ref):
    mixed = jnp.dot(pool_ref[...], wo_ref[0:POOL_WIDTH, :], preferred_element_type=F32)
    mixed = mixed + jnp.dot(gla_ref[...], wo_ref[POOL_WIDTH:POOL_WIDTH + GLA_WIDTH, :],
                            preferred_element_type=F32)
    mixed = mixed + jnp.dot(attn_ref[...], wo_ref[POOL_WIDTH + GLA_WIDTH:, :],
                            preferred_element_type=F32)
    gate1 = mod_ref[:, 2 * D_MODEL:3 * D_MODEL]
    shift2 = mod_ref[:, 3 * D_MODEL:4 * D_MODEL]
    scale2 = mod_ref[:, 4 * D_MODEL:5 * D_MODEL]
    x1 = x_ref[...] + gate1 * mixed
    x1_ref[...] = x1
    h2 = _rms(x1, nf_ref[...]) * (1.0 + scale2) + shift2
    h2_ref[...] = h2
    logits = jnp.dot(h2, wr_ref[...], precision=HIGHEST, preferred_element_type=F32) + br_ref[...]
    route_ref[...] = _route(logits)


def _outproj_call(pool_o, gla_o, attn_o, x, mod3, wo, nf, wr, br):
    tile = lambda w: pl.BlockSpec((TM, w), lambda i: (i, 0))
    return pl.pallas_call(
        _outproj_kernel,
        out_shape=[jax.ShapeDtypeStruct((T_ALL, D_MODEL), F32),
                   jax.ShapeDtypeStruct((T_ALL, D_MODEL), F32),
                   jax.ShapeDtypeStruct((T_ALL, ROUTE_LANES), F32)],
        grid=(N_TOK_TILES,),
        in_specs=[
            tile(POOL_WIDTH), tile(GLA_WIDTH), tile(ATT_WIDTH), tile(D_MODEL),
            pl.BlockSpec((None, 1, 6 * D_MODEL), lambda i: (_mod_row(i), 0, 0)),
            pl.BlockSpec((D_MODEL, D_MODEL), lambda i: (0, 0)),
            pl.BlockSpec((1, D_MODEL), lambda i: (0, 0)),
            pl.BlockSpec((D_MODEL, ROUTE_LANES), lambda i: (0, 0)),
            pl.BlockSpec((1, ROUTE_LANES), lambda i: (0, 0)),
        ],
        out_specs=[tile(D_MODEL), tile(D_MODEL), tile(ROUTE_LANES)],
        compiler_params=pltpu.CompilerParams(
            dimension_semantics=("arbitrary",), vmem_limit_bytes=VMEM_LIMIT),
        name="out_proj_router",
    )(pool_o, gla_o, attn_o, x, mod3, wo, nf, wr, br)


def _row_copy(src_ref, src_row, dst_ref, dst_row, sem):
    return pltpu.make_async_copy(src_ref.at[pl.ds(src_row, 1)], dst_ref.at[pl.ds(dst_row, 1)], sem)


def _dispatch_kernel(pos_ref, h_ref, xs_in_ref, xs_ref, sem):
    del xs_in_ref
    base = pl.program_id(0) * TM * 2

    def issue(r, carry):
        for kk in range(2):
            _row_copy(h_ref, r, xs_ref, pos_ref[base + 2 * r + kk], sem).start()
        return carry

    lax.fori_loop(0, TM, issue, 0)

    def drain(r, carry):
        for kk in range(2):
            _row_copy(h_ref, 0, xs_ref, 0, sem).wait()
        return carry

    lax.fori_loop(0, TM, drain, 0)


def _dispatch_call(pos, h2):
    xs0 = jnp.zeros((MOE_TILES * MOE_TILE, D_MODEL), F32)
    return pl.pallas_call(
        _dispatch_kernel,
        out_shape=jax.ShapeDtypeStruct((MOE_TILES * MOE_TILE, D_MODEL), F32),
        grid_spec=pltpu.PrefetchScalarGridSpec(
            num_scalar_prefetch=1,
            grid=(N_TOK_TILES,),
            in_specs=[pl.BlockSpec((TM, D_MODEL), lambda i, pos: (i, 0)),
                      pl.BlockSpec(memory_space=pl.ANY)],
            out_specs=pl.BlockSpec(memory_space=pl.ANY),
            scratch_shapes=[pltpu.SemaphoreType.DMA(())],
        ),
        input_output_aliases={2: 0},
        compiler_params=pltpu.CompilerParams(
            dimension_semantics=("arbitrary",), vmem_limit_bytes=VMEM_LIMIT),
        name="moe_dispatch",
    )(pos, h2, xs0)


_CAST_ROWS = 256


def _expert_kernel(te_ref, tv_ref, xs_ref, wi_ref, wo_ref, ys_ref, wi_bf, wo_bf):
    i = pl.program_id(0)
    e = te_ref[i]
    prev = te_ref[jnp.maximum(i - 1, 0)]

    @pl.when((i == 0) | (e != prev))
    def _():
        for r in range(0, D_MODEL, _CAST_ROWS):
            wi_bf[r:r + _CAST_ROWS, :] = wi_ref[r:r + _CAST_ROWS, :].astype(BF16)
        for r in range(0, D_EXPERT, _CAST_ROWS):
            wo_bf[r:r + _CAST_ROWS, :] = wo_ref[r:r + _CAST_ROWS, :].astype(BF16)

    @pl.when(tv_ref[i] == 1)
    def _():
        hid = jnp.dot(xs_ref[...].astype(BF16), wi_bf[...], preferred_element_type=F32)
        act = _silu(hid[:, :D_EXPERT]) * hid[:, D_EXPERT:]
        ys_ref[...] = jnp.dot(act.astype(BF16), wo_bf[...], preferred_element_type=F32)

    @pl.when(tv_ref[i] == 0)
    def _():
        ys_ref[...] = jnp.zeros_like(ys_ref)


def _expert_call(tile_expert, tile_valid, xs, w_ei, w_eo):
    return pl.pallas_call(
        _expert_kernel,
        out_shape=jax.ShapeDtypeStruct((MOE_TILES * MOE_TILE, D_MODEL), F32),
        grid_spec=pltpu.PrefetchScalarGridSpec(
            num_scalar_prefetch=2,
            grid=(MOE_TILES,),
            in_specs=[
                pl.BlockSpec((MOE_TILE, D_MODEL), lambda i, te, tv: (i, 0)),
                pl.BlockSpec((None, D_MODEL, 2 * D_EXPERT), lambda i, te, tv: (te[i], 0, 0)),
                pl.BlockSpec((None, D_EXPERT, D_MODEL), lambda i, te, tv: (te[i], 0, 0)),
            ],
            out_specs=pl.BlockSpec((MOE_TILE, D_MODEL), lambda i, te, tv: (i, 0)),
            scratch_shapes=[pltpu.VMEM((D_MODEL, 2 * D_EXPERT), BF16),
                            pltpu.VMEM((D_EXPERT, D_MODEL), BF16)],
        ),
        compiler_params=pltpu.CompilerParams(
            dimension_semantics=("arbitrary",), vmem_limit_bytes=VMEM_LIMIT),
        name="moe_experts",
    )(tile_expert, tile_valid, xs, w_ei, w_eo)


def _combine_kernel(pos_ref, x1_ref, route_ref, mod_ref, ys_ref, o_ref, buf, sem):
    base = pl.program_id(0) * TM * 2

    def issue(r, carry):
        for kk in range(2):
            _row_copy(ys_ref, pos_ref[base + 2 * r + kk], buf.at[kk], r, sem).start()
        return carry

    lax.fori_loop(0, TM, issue, 0)

    def drain(r, carry):
        for kk in range(2):
            _row_copy(ys_ref, 0, buf.at[kk], 0, sem).wait()
        return carry

    lax.fori_loop(0, TM, drain, 0)
    gate2 = mod_ref[:, 5 * D_MODEL:6 * D_MODEL]
    w1 = route_ref[:, 2:3]
    w2 = route_ref[:, 3:4]
    o_ref[...] = x1_ref[...] + gate2 * (w1 * buf[0] + w2 * buf[1])


def _combine_call(pos, x1, route, mod3, ys):
    return pl.pallas_call(
        _combine_kernel,
        out_shape=jax.ShapeDtypeStruct((T_ALL, D_MODEL), F32),
        grid_spec=pltpu.PrefetchScalarGridSpec(
            num_scalar_prefetch=1,
            grid=(N_TOK_TILES,),
            in_specs=[
                pl.BlockSpec((TM, D_MODEL), lambda i, pos: (i, 0)),
                pl.BlockSpec((TM, ROUTE_LANES), lambda i, pos: (i, 0)),
                pl.BlockSpec((None, 1, 6 * D_MODEL), lambda i, pos: (_mod_row(i), 0, 0)),
                pl.BlockSpec(memory_space=pl.ANY),
            ],
            out_specs=pl.BlockSpec((TM, D_MODEL), lambda i, pos: (i, 0)),
            scratch_shapes=[pltpu.VMEM((2, TM, D_MODEL), F32), pltpu.SemaphoreType.DMA(())],
        ),
        compiler_params=pltpu.CompilerParams(
            dimension_semantics=("arbitrary",), vmem_limit_bytes=VMEM_LIMIT),
        name="moe_combine",
    )(pos, x1, route, mod3, ys)


def _dispatch_plan(route):
    ef = route[:, 0:2].astype(jnp.int32).reshape(-1)
    onehot = (ef[:, None] == jnp.arange(N_EXPERTS, dtype=jnp.int32)[None, :]).astype(jnp.int32)
    csum = jnp.cumsum(onehot, axis=0)
    rank = jnp.sum(csum * onehot, axis=1) - 1
    counts = csum[-1]
    ntiles = (counts + MOE_TILE - 1) // MOE_TILE
    tile_end = jnp.cumsum(ntiles)
    tile_start = tile_end - ntiles
    pos = jnp.sum(onehot * tile_start[None, :], axis=1) * MOE_TILE + rank
    n_used = tile_end[-1]
    ti = jnp.arange(MOE_TILES, dtype=jnp.int32)
    expert_of = lambda t: jnp.minimum(jnp.sum((t[:, None] >= tile_end[None, :]).astype(jnp.int32),
                                              axis=1), N_EXPERTS - 1)
    valid = ti < n_used
    tile_expert = expert_of(jnp.where(valid, ti, n_used - 1))
    return pos.astype(jnp.int32), tile_expert.astype(jnp.int32), valid.astype(jnp.int32)


def _reorder_w_in(w):
    lr0 = 512 + 256 + 256 + 512 + 512
    parts = [w[:, :lr0], w[:, lr0 + 2 * GLA_GATE_RANK:], w[:, lr0:lr0 + 2 * GLA_GATE_RANK],
             jnp.zeros((D_MODEL, Z_COLS - Z_LR - 2 * GLA_GATE_RANK), w.dtype)]
    return jnp.concatenate(parts, axis=1).astype(BF16)


def _state_to_blockdiag_t(s):
    st = jnp.swapaxes(s, -1, -2)
    eye = jnp.eye(GLA_HEADS, dtype=s.dtype)
    full = st[:, :, :, :, None, :] * eye[None, None, :, None, :, None]
    return full.reshape(s.shape[0], 2, _V_LANES, _QK_LANES)


def _blockdiag_t_to_state(st):
    b = st.shape[0]
    full = st.reshape(b, 2, GLA_HEADS, GLA_DV, GLA_HEADS, GLA_DK)
    diag = jnp.stack([full[:, :, h, :, h, :] for h in range(GLA_HEADS)], axis=2)
    return jnp.swapaxes(diag, -1, -2)


def kernel(x_prompt, x_sample, cache_k, cache_v, state_gla, c, c_ctx, w_mod, b_mod, norm_mix, w_in,
           w_pool, pool_scale, gla_up, gla_up_b, gla_norm, q_norm, k_norm, w_out, norm_ffn,
           w_group_router, b_group_router, w_expert_router, b_expert_router, w_expert_in,
           w_expert_out):
    x = jnp.concatenate([x_prompt.reshape(T_CTX, D_MODEL), x_sample.reshape(T_LAT, D_MODEL)], axis=0)
    c8 = jnp.concatenate([c, c_ctx[None, :], jnp.zeros((8 - DEC_BATCH - 1, D_MODEL), F32)], axis=0)
    mod = _mod_call(c8, w_mod, b_mod)
    rope = _rope_tables()
    gla_consts = _gla_consts()
    ck = cache_k.reshape(DEC_BATCH, DEPTH, PAST_LEN, ATT_KV_HEADS * HEAD_DIM)
    cv = cache_v.reshape(DEC_BATCH, DEPTH, PAST_LEN, ATT_KV_HEADS * HEAD_DIM)
    lat_blk0 = T_CTX // DEC_SEQ

    new_k, new_v, new_s = [], [], []
    for l in range(DEPTH):
        mod3 = mod[l].reshape(8, 1, 6 * D_MODEL)
        z = _inproj_call(x, mod3, norm_mix[l][None, :], _reorder_w_in(w_in[l]))

        wp = w_pool[l].astype(BF16)
        ps = pool_scale[l][None, :]
        pool_o = jnp.concatenate([_pool_call(z, wp, ps, SEQ, BATCH, 0),
                                  _pool_call(z, wp, ps, DEC_SEQ, DEC_BATCH, lat_blk0)], axis=0)

        up = jnp.zeros((2, 128, _QK_LANES), F32)
        up = up.at[0, 0:GLA_GATE_RANK].set(gla_up[l, 0])
        up = up.at[1, GLA_GATE_RANK:2 * GLA_GATE_RANK].set(gla_up[l, 1])
        upb = gla_up_b[l][:, None, :]
        gn = gla_norm[l][None, :]
        gla_c, st_c = _gla_call(z, up, upb, gn, gla_consts, None, SEQ, BATCH, 0, True)
        (gla_l,) = _gla_call(z, up, upb, gn, gla_consts, _state_to_blockdiag_t(state_gla[:, l]),
                             DEC_SEQ, DEC_BATCH, lat_blk0, False)
        gla_o = jnp.concatenate([gla_c, gla_l], axis=0)
        new_s.append(_blockdiag_t_to_state(st_c))

        qn = q_norm[l][None, :]
        kn = k_norm[l][None, :]
        attn_c, ka, va = _attn_call(z, qn, kn, SEQ, BATCH, 0, False)
        (attn_l,) = _attn_call(z, qn, kn, DEC_SEQ, DEC_BATCH, lat_blk0, True, rope, ck, cv, l)
        attn_o = jnp.concatenate([attn_c, attn_l], axis=0)
        new_k.append(ka.reshape(BATCH, SEQ, ATT_KV_HEADS, HEAD_DIM))
        new_v.append(va.reshape(BATCH, SEQ, ATT_KV_HEADS, HEAD_DIM))

        wr = jnp.concatenate([w_group_router[l], w_expert_router[l],
                              jnp.zeros((D_MODEL, ROUTE_LANES - N_GROUPS - N_EXPERTS), F32)], axis=1)
        br = jnp.concatenate([b_group_router[l], b_expert_router[l],
                              jnp.zeros((ROUTE_LANES - N_GROUPS - N_EXPERTS,), F32)])[None, :]
        x1, h2, route = _outproj_call(pool_o, gla_o, attn_o, x, mod3, w_out[l].astype(BF16),
                                      norm_ffn[l][None, :], wr, br)

        pos, tile_expert, tile_valid = _dispatch_plan(route)
        xs = _dispatch_call(pos, h2)
        ys = _expert_call(tile_expert, tile_valid, xs, w_expert_in[l], w_expert_out[l])
        x = _combine_call(pos, x1, route, mod3, ys)

    y_prompt = x[:T_CTX].reshape(BATCH, SEQ, D_MODEL)
    y_sample = x[T_CTX:].reshape(DEC_BATCH, DEC_SEQ, D_MODEL)
    return (y_prompt, y_sample, jnp.stack(new_k, axis=1), jnp.stack(new_v, axis=1),
            jnp.stack(new_s, axis=1))
```

```python
import functools

import numpy as np
import jax
import jax.numpy as jnp
from jax import lax
from jax.experimental import pallas as pl
from jax.experimental.pallas import tpu as pltpu

F32 = jnp.float32
BF16 = jnp.bfloat16
HIGHEST = lax.Precision.HIGHEST

D_MODEL = 2048
BATCH = 16
SEQ = 256
DEPTH = 2
DEC_BATCH = 4
DEC_SEQ = 1024
PAST_LEN = 512
GRID_W = 64
POOL_WIDTH = 512
POOL_WINDOWS = (2, 4, 8, 16)
POOL_GROUPS = 4
POOL_GROUP_DIM = 128
GLA_WIDTH = 512
GLA_DV = 128
GLA_HEADS = 4
GLA_DK = 64
GLA_GATE_RANK = 16
GLA_TAU = 16.0
GLA_CHUNK = 64
GLA_SUB = 16
ATT_WIDTH = 1024
HEAD_DIM = 128
ATT_HEADS = 8
ATT_KV_HEADS = 2
ROPE_THETA = 10000.0
N_GROUPS = 4
EXPERTS_PER_GROUP = 8
N_EXPERTS = 32
D_EXPERT = 512
EPS = 1e-6

T_CTX = BATCH * SEQ
T_LAT = DEC_BATCH * DEC_SEQ
T_ALL = T_CTX + T_LAT
TM = 256
N_TOK_TILES = T_ALL // TM
CTX_TILES = T_CTX // TM
LAT_TILES_PER_SEQ = DEC_SEQ // TM
CTX_MOD_ROW = DEC_BATCH

Z_POOL = 0
Z_QG = 512
Z_KG = 768
Z_VG = 1024
Z_OG = 1536
Z_QA = 2048
Z_KA = 3072
Z_VA = 3328
Z_LR = 3584
Z_COLS = 3712

MOE_TILE = 256
MOE_SLOTS = T_ALL * 2
MOE_TILES = MOE_SLOTS // MOE_TILE + N_EXPERTS
ROUTE_LANES = 128
VMEM_LIMIT = 56 * 1024 * 1024


def _silu(x):
    return x / (1.0 + jnp.exp(-x))


def _rms(x, g):
    return x * lax.rsqrt(jnp.mean(x * x, axis=-1, keepdims=True) + EPS) * g


def _mod_row(i):
    return jnp.where(i < CTX_TILES, CTX_MOD_ROW, (i - CTX_TILES) // LAT_TILES_PER_SEQ)


def _mod_kernel(c_ref, w_ref, b_ref, o_ref):
    s = _silu(c_ref[...]).astype(BF16)
    o_ref[...] = jnp.dot(s, w_ref[...].astype(BF16), preferred_element_type=F32) + b_ref[...]


def _mod_call(c8, w_mod, b_mod):
    bn = 1536
    return pl.pallas_call(
        _mod_kernel,
        out_shape=jax.ShapeDtypeStruct((DEPTH, 8, 6 * D_MODEL), F32),
        grid=(DEPTH, 6 * D_MODEL // bn),
        in_specs=[
            pl.BlockSpec((8, D_MODEL), lambda l, j: (0, 0)),
            pl.BlockSpec((None, D_MODEL, bn), lambda l, j: (l, 0, j)),
            pl.BlockSpec((None, 1, bn), lambda l, j: (l, 0, j)),
        ],
        out_specs=pl.BlockSpec((None, 8, bn), lambda l, j: (l, 0, j)),
        compiler_params=pltpu.CompilerParams(
            dimension_semantics=("arbitrary", "arbitrary"), vmem_limit_bytes=VMEM_LIMIT),
        name="mod_table",
    )(c8, w_mod, b_mod.reshape(DEPTH, 1, 6 * D_MODEL))


def _ctx_tile(i):
    return jnp.minimum(i, CTX_TILES - 1)


def _lat_tile(i):
    return jnp.maximum(i - CTX_TILES, 0)


def _pick(ctx_ref, lat_ref):
    return jnp.where(pl.program_id(0) < CTX_TILES, ctx_ref[...], lat_ref[...])


def _inproj_kernel(xc_ref, xl_ref, mod_ref, g_ref, w_ref, z_ref):
    y = _rms(_pick(xc_ref, xl_ref), g_ref[...])
    shift = mod_ref[:, 0:D_MODEL]
    scale = mod_ref[:, D_MODEL:2 * D_MODEL]
    h = y * (1.0 + scale) + shift
    z_ref[...] = jnp.dot(h.astype(BF16), w_ref[...], preferred_element_type=F32)


def _inproj_call(xc, xl, mod3, g, w):
    return pl.pallas_call(
        _inproj_kernel,
        out_shape=jax.ShapeDtypeStruct((T_ALL, Z_COLS), F32),
        grid=(N_TOK_TILES,),
        in_specs=[
            pl.BlockSpec((TM, D_MODEL), lambda i: (_ctx_tile(i), 0)),
            pl.BlockSpec((TM, D_MODEL), lambda i: (_lat_tile(i), 0)),
            pl.BlockSpec((None, 1, 6 * D_MODEL), lambda i: (_mod_row(i), 0, 0)),
            pl.BlockSpec((1, D_MODEL), lambda i: (0, 0)),
            pl.BlockSpec((D_MODEL, Z_COLS), lambda i: (0, 0)),
        ],
        out_specs=pl.BlockSpec((TM, Z_COLS), lambda i: (i, 0)),
        compiler_params=pltpu.CompilerParams(
            dimension_semantics=("arbitrary",), vmem_limit_bytes=VMEM_LIMIT),
        name="in_proj",
    )(xc, xl, mod3, g, w)


def _pool_kernel(u_ref, band_ref, invc_ref, wp_ref, ps_ref, o_ref):
    for g in range(POOL_GROUPS):
        cs = slice(g * POOL_GROUP_DIM, (g + 1) * POOL_GROUP_DIM)
        u = u_ref[:, cs]
        hi = u.astype(BF16)
        lo = (u - hi.astype(F32)).astype(BF16)
        band = band_ref[g]
        s = (jnp.dot(band, hi, preferred_element_type=F32)
             + jnp.dot(band, lo, preferred_element_type=F32))
        d = s * invc_ref[:, cs] - u
        y = jnp.dot(d.astype(BF16), wp_ref[g], preferred_element_type=F32)
        o_ref[:, cs] = (y * ps_ref[:, cs]).astype(BF16)


def _pool_consts(L):
    t = np.arange(L)
    band = np.zeros((POOL_GROUPS, L, L), np.float32)
    invc = np.zeros((L, POOL_WIDTH), np.float32)
    for gi, w in enumerate(POOL_WINDOWS):
        start = np.clip(t - w // 2, 0, L)
        end = np.clip(t + w // 2, 0, L)
        band[gi] = (t[None, :] >= start[:, None]) & (t[None, :] < end[:, None])
        invc[:, gi * POOL_GROUP_DIM:(gi + 1) * POOL_GROUP_DIM] = (1.0 / (end - start))[:, None]
    return jnp.asarray(band, BF16), jnp.asarray(invc, F32)


def _pool_call(z, wp, ps, L, nseq, blk0):
    band, invc = _pool_consts(L)
    return pl.pallas_call(
        _pool_kernel,
        out_shape=jax.ShapeDtypeStruct((nseq * L, POOL_WIDTH), BF16),
        grid=(nseq,),
        in_specs=[
            pl.BlockSpec((L, POOL_WIDTH), lambda s: (blk0 + s, Z_POOL // POOL_WIDTH)),
            pl.BlockSpec((POOL_GROUPS, L, L), lambda s: (0, 0, 0)),
            pl.BlockSpec((L, POOL_WIDTH), lambda s: (0, 0)),
            pl.BlockSpec((POOL_GROUPS, POOL_GROUP_DIM, POOL_GROUP_DIM), lambda s: (0, 0, 0)),
            pl.BlockSpec((1, POOL_WIDTH), lambda s: (0, 0)),
        ],
        out_specs=pl.BlockSpec((L, POOL_WIDTH), lambda s: (s, 0)),
        compiler_params=pltpu.CompilerParams(
            dimension_semantics=("arbitrary",), vmem_limit_bytes=VMEM_LIMIT),
        name="pool_mixer_%d" % L,
    )(z, band, invc, wp, ps)


_QK_LANES = GLA_HEADS * GLA_DK
_V_LANES = GLA_HEADS * GLA_DV
_NSUB = GLA_CHUNK // GLA_SUB
_PAIR = GLA_SUB * GLA_SUB


def _gla_consts():
    C, S = GLA_CHUNK, GLA_SUB
    r = np.arange(C)
    cum = np.stack([(r[None, :] <= r[:, None]), (r[None, :] >= r[:, None])]).astype(np.float32)
    lane_qk = np.arange(_QK_LANES)
    col_j, col_h, col_jj = lane_qk // 64, (lane_qk % 64) // S, lane_qk % S
    seg = (lane_qk[:, None] // GLA_DK == col_h[None, :]).astype(np.float32)
    pair = np.arange(_PAIR)
    pi, pj = pair // S, pair % S
    gsum = (np.arange(S)[:, None] == pi[None, :]).astype(np.float32)
    dmask = np.zeros((2, _NSUB, _PAIR, _QK_LANES), np.float32)
    for dr in range(2):
        earlier = (pj <= pi) if dr == 0 else (pj >= pi)
        for I in range(_NSUB):
            dmask[dr, I] = (earlier[:, None] & (pj[:, None] == col_jj[None, :])
                            & (col_j[None, :] == I))
    rows64 = np.arange(4 * S)
    hmask_k = (rows64[:, None] // S == lane_qk[None, :] // GLA_DK).astype(np.float32)
    lane_v = np.arange(_V_LANES)
    hmask_v = (rows64[:, None] // S == lane_v[None, :] // GLA_DV).astype(np.float32)
    bdmask = (lane_v[:, None] // GLA_DV == lane_qk[None, :] // GLA_DK).astype(np.float32)
    return (jnp.asarray(cum), jnp.asarray(seg, BF16), jnp.asarray(gsum, BF16), jnp.asarray(dmask),
            jnp.asarray(hmask_k), jnp.asarray(hmask_v), jnp.asarray(bdmask))


def _gla_kernel(*refs, L, has_state, emit_state):
    (q_ref, k_ref, v_ref, og_ref, lr_ref, up_ref, upb_ref, gn_ref, cum_ref, seg_ref, gsum_ref,
     dmask_ref, hmk_ref, hmv_ref, bdm_ref) = refs[:15]
    pos = 15
    s0_ref = None
    if has_state:
        s0_ref = refs[pos]
        pos += 1
    o_ref = refs[pos]
    pos += 1
    st_out_ref = None
    if emit_state:
        st_out_ref = refs[pos]
        pos += 1
    st_scr, oacc_scr = refs[pos], refs[pos + 1]

    nchunk = L // GLA_CHUNK
    C, S = GLA_CHUNK, GLA_SUB
    nt_dims = (((1,), (1,)), ((), ()))
    tn_dims = (((0,), (0,)), ((), ()))

    for direction in range(2):
        rev = direction == 1
        if has_state:
            st_scr[...] = s0_ref[direction]
        else:
            st_scr[...] = jnp.zeros_like(st_scr)

        def chunk_body(ci, carry, rev=rev, direction=direction):
            c = (nchunk - 1 - ci) if rev else ci
            r0 = pl.multiple_of(c * C, C)
            q = q_ref[pl.ds(r0, C), :] * (GLA_DK ** -0.5)
            k = k_ref[pl.ds(r0, C), :]
            v = v_ref[pl.ds(r0, C), :]
            x = jnp.dot(lr_ref[pl.ds(r0, C), :], up_ref[direction], precision=HIGHEST,
                        preferred_element_type=F32) + upb_ref[direction]
            g = (jnp.minimum(x, 0.0) - jnp.log1p(jnp.exp(-jnp.abs(x)))) / GLA_TAU
            b = jnp.dot(cum_ref[direction], g, precision=HIGHEST, preferred_element_type=F32)
            b_end = b[0:1] if rev else b[C - 1:C]

            st = st_scr[...]
            o = lax.dot_general((q * jnp.exp(b)).astype(BF16), st.astype(BF16), nt_dims,
                                preferred_element_type=F32)

            xs = []
            for I in range(_NSUB):
                q_i, k_i, b_i = q[I * S:(I + 1) * S], k[I * S:(I + 1) * S], b[I * S:(I + 1) * S]
                for i in range(S):
                    dec = jnp.exp(jnp.minimum(b_i[i:i + 1] - b_i, 0.0))
                    xs.append((q_i[i:i + 1] * k_i) * dec)
            xall = jnp.concatenate(xs, axis=0).astype(BF16)
            rall = jnp.dot(xall, seg_ref[...], preferred_element_type=F32)

            a_rows = []
            for I in range(_NSUB):
                rm = (rall[I * _PAIR:(I + 1) * _PAIR] * dmask_ref[direction, I]).astype(BF16)
                a_i = jnp.dot(gsum_ref[...], rm, preferred_element_type=F32)
                first = (I == _NSUB - 1) if rev else (I == 0)
                if not first:
                    ref_row = (I + 1) * S if rev else I * S - 1
                    r_i = b[ref_row:ref_row + 1]
                    qt = q[I * S:(I + 1) * S] * jnp.exp(b[I * S:(I + 1) * S] - r_i)
                    kt = k * jnp.exp(jnp.minimum(r_i - b, 0.0))
                    blocks = []
                    for J in range(_NSUB):
                        earlier = (J > I) if rev else (J < I)
                        if earlier:
                            kj = kt[J * S:(J + 1) * S]
                            blocks.append(jnp.concatenate([kj] * GLA_HEADS, axis=0) * hmk_ref[...])
                        else:
                            blocks.append(jnp.zeros((GLA_HEADS * S, _QK_LANES), F32))
                    rhs = jnp.concatenate(blocks, axis=0).astype(BF16)
                    a_i = a_i + lax.dot_general(qt.astype(BF16), rhs, nt_dims,
                                                preferred_element_type=F32)
                a_rows.append(a_i)
            a = jnp.concatenate(a_rows, axis=0).astype(BF16)

            vblocks = []
            for J in range(_NSUB):
                vj = v[J * S:(J + 1) * S]
                vblocks.append(jnp.concatenate([vj] * GLA_HEADS, axis=0) * hmv_ref[...])
            vbd = jnp.concatenate(vblocks, axis=0).astype(BF16)
            o = o + jnp.dot(a, vbd, preferred_element_type=F32)
            if rev:
                oacc_scr[pl.ds(r0, C), :] = oacc_scr[pl.ds(r0, C), :] + o
            else:
                oacc_scr[pl.ds(r0, C), :] = o

            kl = (k * jnp.exp(b_end - b)).astype(BF16)
            u = lax.dot_general(v.astype(BF16), kl, tn_dims, preferred_element_type=F32)
            st_scr[...] = st * jnp.exp(b_end) + u * bdm_ref[...]
            return carry

        lax.fori_loop(0, nchunk, chunk_body, 0)
        if emit_state:
            st_out_ref[direction] = st_scr[...]

    for h in range(GLA_HEADS):
        cs = slice(h * GLA_DV, (h + 1) * GLA_DV)
        oh = _rms(oacc_scr[:, cs], gn_ref[...])
        o_ref[:, cs] = (oh * _silu(og_ref[:, cs])).astype(BF16)


def _gla_call(z, up, upb, gn, consts, s0t, L, nseq, blk0, emit_state):
    has_state = s0t is not None
    cum, seg, gsum, dmask, hmk, hmv, bdm = consts
    full = lambda shape: pl.BlockSpec(shape, lambda s: (0,) * len(shape))
    in_specs = [
        pl.BlockSpec((L, _QK_LANES), lambda s: (blk0 + s, Z_QG // _QK_LANES)),
        pl.BlockSpec((L, _QK_LANES), lambda s: (blk0 + s, Z_KG // _QK_LANES)),
        pl.BlockSpec((L, _V_LANES), lambda s: (blk0 + s, Z_VG // _V_LANES)),
        pl.BlockSpec((L, _V_LANES), lambda s: (blk0 + s, Z_OG // _V_LANES)),
        pl.BlockSpec((L, 128), lambda s: (blk0 + s, Z_LR // 128)),
        full(up.shape), full(upb.shape), full(gn.shape), full(cum.shape), full(seg.shape),
        full(gsum.shape), full(dmask.shape), full(hmk.shape), full(hmv.shape), full(bdm.shape),
    ]
    args = [z, z, z, z, z, up, upb, gn, cum, seg, gsum, dmask, hmk, hmv, bdm]
    if has_state:
        in_specs.append(pl.BlockSpec((None, 2, _V_LANES, _QK_LANES), lambda s: (s, 0, 0, 0)))
        args.append(s0t)
    out_shape = [jax.ShapeDtypeStruct((nseq * L, GLA_WIDTH), BF16)]
    out_specs = [pl.BlockSpec((L, GLA_WIDTH), lambda s: (s, 0))]
    if emit_state:
        out_shape.append(jax.ShapeDtypeStruct((nseq, 2, _V_LANES, _QK_LANES), F32))
        out_specs.append(pl.BlockSpec((None, 2, _V_LANES, _QK_LANES), lambda s: (s, 0, 0, 0)))
    return pl.pallas_call(
        functools.partial(_gla_kernel, L=L, has_state=has_state, emit_state=emit_state),
        out_shape=out_shape,
        grid=(nseq,),
        in_specs=in_specs,
        out_specs=out_specs,
        scratch_shapes=[pltpu.VMEM((_V_LANES, _QK_LANES), F32), pltpu.VMEM((L, _V_LANES), F32)],
        compiler_params=pltpu.CompilerParams(
            dimension_semantics=("arbitrary",), vmem_limit_bytes=VMEM_LIMIT),
        name="gla_mixer_%d" % L,
    )(*args)


_TQ = 256


def _rope(x, cos, sin_signed, first_of_pair):
    partner = jnp.where(first_of_pair, pltpu.roll(x, HEAD_DIM - 32, 1), pltpu.roll(x, 32, 1))
    return x * cos + partner * sin_signed


def _attn_kernel(*refs, L, latent):
    q_ref, k_ref, v_ref, qn_ref, kn_ref = refs[:5]
    if latent:
        cos_ref, sin_ref, ck_ref, cv_ref, o_ref = refs[5:10]
    else:
        o_ref, ka_ref, va_ref = refs[5:8]
    G = ATT_HEADS // ATT_KV_HEADS
    scale = HEAD_DIM ** -0.5
    kn = _rms(k_ref[...], kn_ref[...])
    v = v_ref[...]
    if latent:
        lane = lax.broadcasted_iota(jnp.int32, (1, HEAD_DIM), 1)
        first_of_pair = (lane % 64) < 32
        kr = _rope(kn, cos_ref[...], sin_ref[...], first_of_pair)
        keys = jnp.concatenate([ck_ref[...], kr], axis=0).astype(BF16)
        vals = jnp.concatenate([cv_ref[...], v], axis=0).astype(BF16)
    else:
        ka_ref[...] = kn
        va_ref[...] = v
        keys = kn.astype(BF16)
        vals = v.astype(BF16)
    for g in range(G):
        cs = slice(g * HEAD_DIM, (g + 1) * HEAD_DIM)
        for qb in range(L // _TQ):
            rs = slice(qb * _TQ, (qb + 1) * _TQ)
            qn = _rms(q_ref[rs, cs], qn_ref[...])
            if latent:
                qn = _rope(qn, cos_ref[rs, :], sin_ref[rs, :], first_of_pair)
            s = lax.dot_general(qn.astype(BF16), keys, (((1,), (1,)), ((), ())),
                                preferred_element_type=F32) * scale
            p = jnp.exp(s - jnp.max(s, axis=-1, keepdims=True))
            p = p / jnp.sum(p, axis=-1, keepdims=True)
            o = jnp.dot(p.astype(BF16), vals, preferred_element_type=F32)
            o_ref[rs, cs] = o.astype(BF16)


def _rope_tables():
    rows = DEC_SEQ // GRID_W
    half = HEAD_DIM // 2
    quarter = half // 2
    freqs = 1.0 / (ROPE_THETA ** (jnp.arange(quarter, dtype=jnp.float32) / quarter))
    row = jnp.repeat(jnp.arange(rows), GRID_W).astype(jnp.float32)
    col = jnp.tile(jnp.arange(GRID_W), rows).astype(jnp.float32)

    def tab(p):
        ang = p[:, None] * freqs[None, :]
        c, s = jnp.cos(ang), jnp.sin(ang)
        return jnp.concatenate([c, c], axis=-1), jnp.concatenate([-s, s], axis=-1)

    cr, sr = tab(row)
    cc, sc = tab(col)
    return jnp.concatenate([cr, cc], axis=-1), jnp.concatenate([sr, sc], axis=-1)


def _attn_call(z, qn, kn, L, nseq, blk0, latent, rope=None, ck=None, cv=None, layer=0):
    G = ATT_HEADS // ATT_KV_HEADS
    qw = G * HEAD_DIM
    in_specs = [
        pl.BlockSpec((L, qw), lambda s, kv: (blk0 + s, Z_QA // qw + kv)),
        pl.BlockSpec((L, HEAD_DIM), lambda s, kv: (blk0 + s, Z_KA // HEAD_DIM + kv)),
        pl.BlockSpec((L, HEAD_DIM), lambda s, kv: (blk0 + s, Z_VA // HEAD_DIM + kv)),
        pl.BlockSpec((1, HEAD_DIM), lambda s, kv: (0, 0)),
        pl.BlockSpec((1, HEAD_DIM), lambda s, kv: (0, 0)),
    ]
    args = [z, z, z, qn, kn]
    out_shape = [jax.ShapeDtypeStruct((nseq * L, ATT_WIDTH), BF16)]
    out_specs = [pl.BlockSpec((L, qw), lambda s, kv: (s, kv))]
    if latent:
        in_specs += [
            pl.BlockSpec((L, HEAD_DIM), lambda s, kv: (0, 0)),
            pl.BlockSpec((L, HEAD_DIM), lambda s, kv: (0, 0)),
            pl.BlockSpec((None, None, PAST_LEN, HEAD_DIM), lambda s, kv: (s, layer, 0, kv)),
            pl.BlockSpec((None, None, PAST_LEN, HEAD_DIM), lambda s, kv: (s, layer, 0, kv)),
        ]
        args += [rope[0], rope[1], ck, cv]
    else:
        kv_shape = jax.ShapeDtypeStruct((nseq, L, ATT_KV_HEADS * HEAD_DIM), F32)
        out_shape += [kv_shape, kv_shape]
        kv_spec = pl.BlockSpec((None, L, HEAD_DIM), lambda s, kv: (s, 0, kv))
        out_specs += [kv_spec, kv_spec]
    return pl.pallas_call(
        functools.partial(_attn_kernel, L=L, latent=latent),
        out_shape=out_shape,
        grid=(nseq, ATT_KV_HEADS),
        in_specs=in_specs,
        out_specs=out_specs,
        compiler_params=pltpu.CompilerParams(
            dimension_semantics=("arbitrary", "arbitrary"), vmem_limit_bytes=VMEM_LIMIT),
        name="attn_mixer_%d" % L,
    )(*args)


def _route(logits):
    lane = lax.broadcasted_iota(jnp.int32, logits.shape, 1)
    lane_f = lane.astype(F32)
    neg = jnp.float32(-jnp.inf)
    big = jnp.float32(ROUTE_LANES)
    is_g = lane < N_GROUPS
    gl = jnp.where(is_g, logits, neg)
    gmax = jnp.max(gl, axis=-1, keepdims=True)
    g_idx = jnp.min(jnp.where(gl == gmax, lane_f, big), axis=-1, keepdims=True)
    g_w = 1.0 / jnp.sum(jnp.where(is_g, jnp.exp(gl - gmax), 0.0), axis=-1, keepdims=True)
    e_lane = lane - N_GROUPS
    in_group = (e_lane >= 0) & (e_lane < N_EXPERTS) & (
        jnp.right_shift(e_lane, 3).astype(F32) == g_idx)
    el = jnp.where(in_group, logits, neg)
    m1 = jnp.max(el, axis=-1, keepdims=True)
    i1 = jnp.min(jnp.where(el == m1, lane_f, big), axis=-1, keepdims=True)
    el2 = jnp.where(lane_f == i1, neg, el)
    m2 = jnp.max(el2, axis=-1, keepdims=True)
    i2 = jnp.min(jnp.where(el2 == m2, lane_f, big), axis=-1, keepdims=True)
    zsum = jnp.sum(jnp.where(in_group, jnp.exp(el - m1), 0.0), axis=-1, keepdims=True)
    p1 = 1.0 / zsum
    p2 = jnp.exp(m2 - m1) / zsum
    w1 = p1 / (p1 + p2) * g_w
    w2 = p2 / (p1 + p2) * g_w
    out = jnp.where(lane == 0, i1 - N_GROUPS, 0.0)
    out = jnp.where(lane == 1, i2 - N_GROUPS, out)
    out = jnp.where(lane == 2, w1, out)
    out = jnp.where(lane == 3, w2, out)
    return out


def _outproj_kernel(pc_ref, pl_ref, gc_ref, gl_ref, ac_ref, al_ref, xc_ref, xl_ref, mod_ref, wo_ref,
                    nf_ref, wr_ref, br_ref, x1_ref, h2_ref, route_ref):
    mixed = jnp.dot(_pick(pc_ref, pl_ref), wo_ref[0:POOL_WIDTH, :], preferred_element_type=F32)
    mixed = mixed + jnp.dot(_pick(gc_ref, gl_ref), wo_ref[POOL_WIDTH:POOL_WIDTH + GLA_WIDTH, :],
                            preferred_element_type=F32)
    mixed = mixed + jnp.dot(_pick(ac_ref, al_ref), wo_ref[POOL_WIDTH + GLA_WIDTH:, :],
                            preferred_element_type=F32)
    gate1 = mod_ref[:, 2 * D_MODEL:3 * D_MODEL]
    shift2 = mod_ref[:, 3 * D_MODEL:4 * D_MODEL]
    scale2 = mod_ref[:, 4 * D_MODEL:5 * D_MODEL]
    x1 = _pick(xc_ref, xl_ref) + gate1 * mixed
    x1_ref[...] = x1
    h2 = _rms(x1, nf_ref[...]) * (1.0 + scale2) + shift2
    h2_ref[...] = h2
    h_hi = h2.astype(BF16)
    h_lo = (h2 - h_hi.astype(F32)).astype(BF16)
    l2 = jnp.dot(h_hi, wr_ref[...], preferred_element_type=F32)
    logits = (l2[:, :ROUTE_LANES] + l2[:, ROUTE_LANES:] + br_ref[...]
              + jnp.dot(h_lo, wr_ref[:, :ROUTE_LANES], preferred_element_type=F32))
    route_ref[...] = _route(logits)


def _outproj_call(pool_o, gla_o, attn_o, x, mod3, wo, nf, wr, br):
    tile = lambda w: pl.BlockSpec((TM, w), lambda i: (i, 0))
    ctx = lambda w: pl.BlockSpec((TM, w), lambda i: (_ctx_tile(i), 0))
    lat = lambda w: pl.BlockSpec((TM, w), lambda i: (_lat_tile(i), 0))
    return pl.pallas_call(
        _outproj_kernel,
        out_shape=[jax.ShapeDtypeStruct((T_ALL, D_MODEL), F32),
                   jax.ShapeDtypeStruct((T_ALL, D_MODEL), F32),
                   jax.ShapeDtypeStruct((T_ALL, ROUTE_LANES), F32)],
        grid=(N_TOK_TILES,),
        in_specs=[
            ctx(POOL_WIDTH), lat(POOL_WIDTH), ctx(GLA_WIDTH), lat(GLA_WIDTH),
            ctx(ATT_WIDTH), lat(ATT_WIDTH), ctx(D_MODEL), lat(D_MODEL),
            pl.BlockSpec((None, 1, 6 * D_MODEL), lambda i: (_mod_row(i), 0, 0)),
            pl.BlockSpec((D_MODEL, D_MODEL), lambda i: (0, 0)),
            pl.BlockSpec((1, D_MODEL), lambda i: (0, 0)),
            pl.BlockSpec((D_MODEL, 2 * ROUTE_LANES), lambda i: (0, 0)),
            pl.BlockSpec((1, ROUTE_LANES), lambda i: (0, 0)),
        ],
        out_specs=[tile(D_MODEL), tile(D_MODEL), tile(ROUTE_LANES)],
        compiler_params=pltpu.CompilerParams(
            dimension_semantics=("arbitrary",), vmem_limit_bytes=VMEM_LIMIT),
        name="out_proj_router",
    )(*pool_o, *gla_o, *attn_o, *x, mod3, wo, nf, wr, br)


def _row_copy(src_ref, src_row, dst_ref, dst_row, sem):
    return pltpu.make_async_copy(src_ref.at[pl.ds(src_row, 1)], dst_ref.at[pl.ds(dst_row, 1)], sem)


_CAST_ROWS = 256
_MOE_STEPS = MOE_TILES + 1


def _expert_kernel(pos_ref, te_ref, nreal_ref, nused_ref, h_ref, wi_ref, wo_ref, y2_ref,
                   inv_scr, xbuf, ybuf, wi_bf, wo_bf, gsem, ssem):
    i = pl.program_id(0)
    n_used = nused_ref[0]

    def gather(tile, slot, start):
        def body(r, carry):
            if start:
                s = inv_scr[tile * MOE_TILE + r]
                _row_copy(h_ref, lax.shift_right_logical(s, 1), xbuf.at[slot], r, gsem.at[slot]).start()
            else:
                _row_copy(h_ref, 0, xbuf.at[slot], 0, gsem.at[slot]).wait()
            return carry
        lax.fori_loop(0, nreal_ref[tile], body, 0)

    def scatter(tile, slot, start):
        def body(r, carry):
            if start:
                s = inv_scr[tile * MOE_TILE + r]
                _row_copy(ybuf.at[slot], r, y2_ref, s, ssem.at[slot]).start()
            else:
                _row_copy(ybuf.at[slot], 0, y2_ref, 0, ssem.at[slot]).wait()
            return carry
        lax.fori_loop(0, nreal_ref[tile], body, 0)

    @pl.when(i == 0)
    def _():
        def invert(s, carry):
            inv_scr[pos_ref[s]] = s
            return carry
        lax.fori_loop(0, MOE_SLOTS, invert, 0)
        xbuf[...] = jnp.zeros_like(xbuf)
        gather(0, 0, True)

    slot = i % 2

    @pl.when(i < n_used)
    def _():
        gather(i, slot, False)

        @pl.when(i + 1 < n_used)
        def _():
            gather(i + 1, 1 - slot, True)

        @pl.when(i >= 2)
        def _():
            scatter(i - 2, slot, False)

        e = te_ref[i]
        prev = te_ref[jnp.maximum(i - 1, 0)]

        @pl.when((i == 0) | (e != prev))
        def _():
            for r in range(0, D_MODEL, _CAST_ROWS):
                wi_bf[r:r + _CAST_ROWS, :] = wi_ref[r:r + _CAST_ROWS, :].astype(BF16)
            for r in range(0, D_EXPERT, _CAST_ROWS):
                wo_bf[r:r + _CAST_ROWS, :] = wo_ref[r:r + _CAST_ROWS, :].astype(BF16)

        hid = jnp.dot(xbuf[slot].astype(BF16), wi_bf[...], preferred_element_type=F32)
        act = _silu(hid[:, :D_EXPERT]) * hid[:, D_EXPERT:]
        ybuf[slot] = jnp.dot(act.astype(BF16), wo_bf[...], preferred_element_type=F32)
        scatter(i, slot, True)

    @pl.when(i == n_used)
    def _():
        scatter(i - 1, 1 - slot, False)

        @pl.when(i >= 2)
        def _():
            scatter(i - 2, slot, False)


def _expert_call(plan, h2, w_ei, w_eo, layer):
    pos, tile_expert, nreal, n_used = plan
    return pl.pallas_call(
        _expert_kernel,
        out_shape=jax.ShapeDtypeStruct((MOE_SLOTS, D_MODEL), F32),
        grid_spec=pltpu.PrefetchScalarGridSpec(
            num_scalar_prefetch=4,
            grid=(_MOE_STEPS,),
            in_specs=[
                pl.BlockSpec(memory_space=pl.ANY),
                pl.BlockSpec((None, None, D_MODEL, 2 * D_EXPERT),
                             lambda i, pos, te, nr, nu: (layer, te[i], 0, 0)),
                pl.BlockSpec((None, None, D_EXPERT, D_MODEL),
                             lambda i, pos, te, nr, nu: (layer, te[i], 0, 0)),
            ],
            out_specs=pl.BlockSpec(memory_space=pl.ANY),
            scratch_shapes=[
                pltpu.SMEM((_MOE_STEPS * MOE_TILE,), jnp.int32),
                pltpu.VMEM((2, MOE_TILE, D_MODEL), F32),
                pltpu.VMEM((2, MOE_TILE, D_MODEL), F32),
                pltpu.VMEM((D_MODEL, 2 * D_EXPERT), BF16),
                pltpu.VMEM((D_EXPERT, D_MODEL), BF16),
                pltpu.SemaphoreType.DMA((2,)),
                pltpu.SemaphoreType.DMA((2,)),
            ],
        ),
        compiler_params=pltpu.CompilerParams(
            dimension_semantics=("arbitrary",), vmem_limit_bytes=VMEM_LIMIT),
        name="moe_experts",
    )(pos, tile_expert, nreal, n_used, h2, w_ei, w_eo)


def _combine_kernel(x1_ref, route_ref, mod_ref, y2_ref, oc_ref, ol_ref):
    gate2 = mod_ref[:, 5 * D_MODEL:6 * D_MODEL]
    w1 = route_ref[:, 2:3]
    w2 = route_ref[:, 3:4]
    y = w1 * y2_ref[:, :D_MODEL] + w2 * y2_ref[:, D_MODEL:]
    out = x1_ref[...] + gate2 * y
    is_ctx = pl.program_id(0) < CTX_TILES

    @pl.when(is_ctx)
    def _():
        oc_ref[...] = out

    @pl.when(jnp.logical_not(is_ctx))
    def _():
        ol_ref[...] = out


def _combine_call(x1, route, mod3, y2):
    return pl.pallas_call(
        _combine_kernel,
        out_shape=[jax.ShapeDtypeStruct((T_CTX, D_MODEL), F32),
                   jax.ShapeDtypeStruct((T_LAT, D_MODEL), F32)],
        grid=(N_TOK_TILES,),
        in_specs=[
            pl.BlockSpec((TM, D_MODEL), lambda i: (i, 0)),
            pl.BlockSpec((TM, ROUTE_LANES), lambda i: (i, 0)),
            pl.BlockSpec((None, 1, 6 * D_MODEL), lambda i: (_mod_row(i), 0, 0)),
            pl.BlockSpec((TM, 2 * D_MODEL), lambda i: (i, 0)),
        ],
        out_specs=[pl.BlockSpec((TM, D_MODEL), lambda i: (_ctx_tile(i), 0)),
                   pl.BlockSpec((TM, D_MODEL), lambda i: (_lat_tile(i), 0))],
        compiler_params=pltpu.CompilerParams(
            dimension_semantics=("arbitrary",), vmem_limit_bytes=VMEM_LIMIT),
        name="moe_combine",
    )(x1, route, mod3, y2.reshape(T_ALL, 2 * D_MODEL))


def _dispatch_plan(route):
    ef = route[:, 0:2].astype(jnp.int32).reshape(-1)
    onehot = (ef[:, None] == jnp.arange(N_EXPERTS, dtype=jnp.int32)[None, :]).astype(jnp.int32)
    csum = jnp.cumsum(onehot, axis=0)
    rank = jnp.sum(csum * onehot, axis=1) - 1
    counts = csum[-1]
    ntiles = (counts + MOE_TILE - 1) // MOE_TILE
    tile_end = jnp.cumsum(ntiles)
    tile_start = tile_end - ntiles
    pos = jnp.sum(onehot * tile_start[None, :], axis=1) * MOE_TILE + rank
    n_used = tile_end[-1]
    ti = jnp.arange(_MOE_STEPS, dtype=jnp.int32)
    valid = ti < n_used
    tc = jnp.where(valid, ti, n_used - 1)
    tile_expert = jnp.minimum(jnp.sum((tc[:, None] >= tile_end[None, :]).astype(jnp.int32), axis=1),
                              N_EXPERTS - 1)
    sel = (tile_expert[:, None] == jnp.arange(N_EXPERTS, dtype=jnp.int32)[None, :]).astype(jnp.int32)
    start_of = jnp.sum(sel * tile_start[None, :], axis=1)
    cnt_of = jnp.sum(sel * counts[None, :], axis=1)
    nreal = jnp.where(valid, jnp.clip(cnt_of - (tc - start_of) * MOE_TILE, 0, MOE_TILE), 0)
    return (pos.astype(jnp.int32), tile_expert.astype(jnp.int32), nreal.astype(jnp.int32),
            n_used.astype(jnp.int32).reshape(1))


def _reorder_w_in(w):
    lr0 = 512 + 256 + 256 + 512 + 512
    parts = [w[:, :lr0], w[:, lr0 + 2 * GLA_GATE_RANK:], w[:, lr0:lr0 + 2 * GLA_GATE_RANK],
             jnp.zeros((D_MODEL, Z_COLS - Z_LR - 2 * GLA_GATE_RANK), w.dtype)]
    return jnp.concatenate(parts, axis=1).astype(BF16)


def _state_to_blockdiag_t(s):
    st = jnp.swapaxes(s, -1, -2)
    eye = jnp.eye(GLA_HEADS, dtype=s.dtype)
    full = st[:, :, :, :, None, :] * eye[None, None, :, None, :, None]
    return full.reshape(s.shape[0], 2, _V_LANES, _QK_LANES)


def _blockdiag_t_to_state(st):
    b = st.shape[0]
    full = st.reshape(b, 2, GLA_HEADS, GLA_DV, GLA_HEADS, GLA_DK)
    diag = jnp.stack([full[:, :, h, :, h, :] for h in range(GLA_HEADS)], axis=2)
    return jnp.swapaxes(diag, -1, -2)


def kernel(x_prompt, x_sample, cache_k, cache_v, state_gla, c, c_ctx, w_mod, b_mod, norm_mix, w_in,
           w_pool, pool_scale, gla_up, gla_up_b, gla_norm, q_norm, k_norm, w_out, norm_ffn,
           w_group_router, b_group_router, w_expert_router, b_expert_router, w_expert_in,
           w_expert_out):
    x = (x_prompt.reshape(T_CTX, D_MODEL), x_sample.reshape(T_LAT, D_MODEL))
    c8 =jnp.concatenate([c, c_ctx[None, :], jnp.zeros((8 - DEC_BATCH - 1, D_MODEL), F32)], axis=0)
    mod = _mod_call(c8, w_mod, b_mod)
    rope = _rope_tables()
    gla_consts = _gla_consts()
    ck = cache_k.reshape(DEC_BATCH, DEPTH, PAST_LEN, ATT_KV_HEADS * HEAD_DIM)
    cv = cache_v.reshape(DEC_BATCH, DEPTH, PAST_LEN, ATT_KV_HEADS * HEAD_DIM)
    lat_blk0 = T_CTX // DEC_SEQ

    new_k, new_v, new_s = [], [], []
    for l in range(DEPTH):
        mod3 = mod[l].reshape(8, 1, 6 * D_MODEL)
        z = _inproj_call(x[0], x[1], mod3, norm_mix[l][None, :], _reorder_w_in(w_in[l]))

        wp = w_pool[l].astype(BF16)
        ps = pool_scale[l][None, :]
        pool_o = (_pool_call(z, wp, ps, SEQ, BATCH, 0),
                  _pool_call(z, wp, ps, DEC_SEQ, DEC_BATCH, lat_blk0))

        up = jnp.zeros((2, 128, _QK_LANES), F32)
        up = up.at[0, 0:GLA_GATE_RANK].set(gla_up[l, 0])
        up = up.at[1, GLA_GATE_RANK:2 * GLA_GATE_RANK].set(gla_up[l, 1])
        upb = gla_up_b[l][:, None, :]
        gn = gla_norm[l][None, :]
        gla_c, st_c = _gla_call(z, up, upb, gn, gla_consts, None, SEQ, BATCH, 0, True)
        (gla_l,) = _gla_call(z, up, upb, gn, gla_consts, _state_to_blockdiag_t(state_gla[:, l]),
                             DEC_SEQ, DEC_BATCH, lat_blk0, False)
        gla_o = (gla_c, gla_l)
        new_s.append(_blockdiag_t_to_state(st_c))

        qn = q_norm[l][None, :]
        kn = k_norm[l][None, :]
        attn_c, ka, va = _attn_call(z, qn, kn, SEQ, BATCH, 0, False)
        (attn_l,) = _attn_call(z, qn, kn, DEC_SEQ, DEC_BATCH, lat_blk0, True, rope, ck, cv, l)
        attn_o = (attn_c, attn_l)
        new_k.append(ka.reshape(BATCH, SEQ, ATT_KV_HEADS, HEAD_DIM))
        new_v.append(va.reshape(BATCH, SEQ, ATT_KV_HEADS, HEAD_DIM))

        wr = jnp.concatenate([w_group_router[l], w_expert_router[l],
                              jnp.zeros((D_MODEL, ROUTE_LANES - N_GROUPS - N_EXPERTS), F32)], axis=1)
        wr_hi = wr.astype(BF16)
        wr = jnp.concatenate([wr_hi, (wr - wr_hi.astype(F32)).astype(BF16)], axis=1)
        br = jnp.concatenate([b_group_router[l], b_expert_router[l],
                              jnp.zeros((ROUTE_LANES - N_GROUPS - N_EXPERTS,), F32)])[None, :]
        x1, h2, route = _outproj_call(pool_o, gla_o, attn_o, x, mod3, w_out[l].astype(BF16),
                                      norm_ffn[l][None, :], wr, br)

        y2 = _expert_call(_dispatch_plan(route), h2, w_expert_in, w_expert_out, l)
        x = _combine_call(x1, route, mod3, y2)

    y_prompt = x[0].reshape(BATCH, SEQ, D_MODEL)
    y_sample = x[1].reshape(DEC_BATCH, DEC_SEQ, D_MODEL)
    return (y_prompt, y_sample, jnp.stack(new_k, axis=1), jnp.stack(new_v, axis=1),
            jnp.stack(new_s, axis=1))
```

```python
import functools

import numpy as np
import jax
import jax.numpy as jnp
from jax import lax
from jax.experimental import pallas as pl
from jax.experimental.pallas import tpu as pltpu

F32 = jnp.float32
BF16 = jnp.bfloat16
HIGHEST = lax.Precision.HIGHEST

D_MODEL = 2048
BATCH = 16
SEQ = 256
DEPTH = 2
DEC_BATCH = 4
DEC_SEQ = 1024
PAST_LEN = 512
GRID_W = 64
POOL_WIDTH = 512
POOL_WINDOWS = (2, 4, 8, 16)
POOL_GROUPS = 4
POOL_GROUP_DIM = 128
GLA_WIDTH = 512
GLA_DV = 128
GLA_HEADS = 4
GLA_DK = 64
GLA_GATE_RANK = 16
GLA_TAU = 16.0
GLA_CHUNK = 64
GLA_SUB = 16
ATT_WIDTH = 1024
HEAD_DIM = 128
ATT_HEADS = 8
ATT_KV_HEADS = 2
ROPE_THETA = 10000.0
N_GROUPS = 4
EXPERTS_PER_GROUP = 8
N_EXPERTS = 32
D_EXPERT = 512
EPS = 1e-6

T_CTX = BATCH * SEQ
T_LAT = DEC_BATCH * DEC_SEQ
T_ALL = T_CTX + T_LAT
TM = 256
N_TOK_TILES = T_ALL // TM
CTX_TILES = T_CTX // TM
LAT_TILES_PER_SEQ = DEC_SEQ // TM
CTX_MOD_ROW = DEC_BATCH

Z_POOL = 0
Z_QG = 512
Z_KG = 768
Z_VG = 1024
Z_OG = 1536
Z_QA = 2048
Z_KA = 3072
Z_VA = 3328
Z_LR = 3584
Z_COLS = 3712

MOE_TILE = 256
MOE_SLOTS = T_ALL * 2
MOE_TILES = MOE_SLOTS // MOE_TILE + N_EXPERTS
ROUTE_LANES = 128
VMEM_LIMIT = 56 * 1024 * 1024


def _silu(x):
    return x / (1.0 + jnp.exp(-x))


def _rms(x, g):
    return x * lax.rsqrt(jnp.mean(x * x, axis=-1, keepdims=True) + EPS) * g


def _mod_row(i):
    return jnp.where(i < CTX_TILES, CTX_MOD_ROW, (i - CTX_TILES) // LAT_TILES_PER_SEQ)


def _mod_kernel(c_ref, w_ref, b_ref, o_ref):
    s = _silu(c_ref[...]).astype(BF16)
    o_ref[...] = jnp.dot(s, w_ref[...].astype(BF16), preferred_element_type=F32) + b_ref[...]


def _mod_call(c8, w_mod, b_mod):
    bn = 1536
    return pl.pallas_call(
        _mod_kernel,
        out_shape=jax.ShapeDtypeStruct((DEPTH, 8, 6 * D_MODEL), F32),
        grid=(DEPTH, 6 * D_MODEL // bn),
        in_specs=[
            pl.BlockSpec((8, D_MODEL), lambda l, j: (0, 0)),
            pl.BlockSpec((None, D_MODEL, bn), lambda l, j: (l, 0, j)),
            pl.BlockSpec((None, 1, bn), lambda l, j: (l, 0, j)),
        ],
        out_specs=pl.BlockSpec((None, 8, bn), lambda l, j: (l, 0, j)),
        compiler_params=pltpu.CompilerParams(
            dimension_semantics=("arbitrary", "arbitrary"), vmem_limit_bytes=VMEM_LIMIT),
        name="mod_table",
    )(c8, w_mod, b_mod.reshape(DEPTH, 1, 6 * D_MODEL))


def _ctx_tile(i):
    return jnp.minimum(i, CTX_TILES - 1)


def _lat_tile(i):
    return jnp.maximum(i - CTX_TILES, 0)


def _pick(ctx_ref, lat_ref):
    return jnp.where(pl.program_id(0) < CTX_TILES, ctx_ref[...], lat_ref[...])


def _inproj_kernel(xc_ref, xl_ref, mod_ref, g_ref, w_ref, z_ref):
    y = _rms(_pick(xc_ref, xl_ref), g_ref[...])
    shift = mod_ref[:, 0:D_MODEL]
    scale = mod_ref[:, D_MODEL:2 * D_MODEL]
    h = y * (1.0 + scale) + shift
    z_ref[...] = jnp.dot(h.astype(BF16), w_ref[...], preferred_element_type=F32)


def _inproj_call(xc, xl, mod3, g, w):
    return pl.pallas_call(
        _inproj_kernel,
        out_shape=jax.ShapeDtypeStruct((T_ALL, Z_COLS), F32),
        grid=(N_TOK_TILES,),
        in_specs=[
            pl.BlockSpec((TM, D_MODEL), lambda i: (_ctx_tile(i), 0)),
            pl.BlockSpec((TM, D_MODEL), lambda i: (_lat_tile(i), 0)),
            pl.BlockSpec((None, 1, 6 * D_MODEL), lambda i: (_mod_row(i), 0, 0)),
            pl.BlockSpec((1, D_MODEL), lambda i: (0, 0)),
            pl.BlockSpec((D_MODEL, Z_COLS), lambda i: (0, 0)),
        ],
        out_specs=pl.BlockSpec((TM, Z_COLS), lambda i: (i, 0)),
        compiler_params=pltpu.CompilerParams(
            dimension_semantics=("arbitrary",), vmem_limit_bytes=VMEM_LIMIT),
        name="in_proj",
    )(xc, xl, mod3, g, w)


def _pool_kernel(u_ref, band_ref, invc_ref, wp_ref, ps_ref, o_ref):
    for g in range(POOL_GROUPS):
        cs = slice(g * POOL_GROUP_DIM, (g + 1) * POOL_GROUP_DIM)
        u = u_ref[:, cs]
        hi = u.astype(BF16)
        lo = (u - hi.astype(F32)).astype(BF16)
        band = band_ref[g]
        s = (jnp.dot(band, hi, preferred_element_type=F32)
             + jnp.dot(band, lo, preferred_element_type=F32))
        d = s * invc_ref[:, cs] - u
        y = jnp.dot(d.astype(BF16), wp_ref[g], preferred_element_type=F32)
        o_ref[:, cs] = (y * ps_ref[:, cs]).astype(BF16)


def _pool_consts(L):
    t = np.arange(L)
    band = np.zeros((POOL_GROUPS, L, L), np.float32)
    invc = np.zeros((L, POOL_WIDTH), np.float32)
    for gi, w in enumerate(POOL_WINDOWS):
        start = np.clip(t - w // 2, 0, L)
        end = np.clip(t + w // 2, 0, L)
        band[gi] = (t[None, :] >= start[:, None]) & (t[None, :] < end[:, None])
        invc[:, gi * POOL_GROUP_DIM:(gi + 1) * POOL_GROUP_DIM] = (1.0 / (end - start))[:, None]
    return jnp.asarray(band, BF16), jnp.asarray(invc, F32)


def _pool_call(z, wp, ps, L, nseq, blk0):
    band, invc = _pool_consts(L)
    return pl.pallas_call(
        _pool_kernel,
        out_shape=jax.ShapeDtypeStruct((nseq * L, POOL_WIDTH), BF16),
        grid=(nseq,),
        in_specs=[
            pl.BlockSpec((L, POOL_WIDTH), lambda s: (blk0 + s, Z_POOL // POOL_WIDTH)),
            pl.BlockSpec((POOL_GROUPS, L, L), lambda s: (0, 0, 0)),
            pl.BlockSpec((L, POOL_WIDTH), lambda s: (0, 0)),
            pl.BlockSpec((POOL_GROUPS, POOL_GROUP_DIM, POOL_GROUP_DIM), lambda s: (0, 0, 0)),
            pl.BlockSpec((1, POOL_WIDTH), lambda s: (0, 0)),
        ],
        out_specs=pl.BlockSpec((L, POOL_WIDTH), lambda s: (s, 0)),
        compiler_params=pltpu.CompilerParams(
            dimension_semantics=("arbitrary",), vmem_limit_bytes=VMEM_LIMIT),
        name="pool_mixer_%d" % L,
    )(z, band, invc, wp, ps)


_QK_LANES = GLA_HEADS * GLA_DK
_V_LANES = GLA_HEADS * GLA_DV
_NSUB = GLA_CHUNK // GLA_SUB
_PAIR = GLA_SUB * GLA_SUB


def _gla_consts():
    C, S = GLA_CHUNK, GLA_SUB
    r = np.arange(C)
    cum = np.stack([(r[None, :] <= r[:, None]), (r[None, :] >= r[:, None])]).astype(np.float32)
    lane_qk = np.arange(_QK_LANES)
    col_j, col_h, col_jj = lane_qk // 64, (lane_qk % 64) // S, lane_qk % S
    seg = (lane_qk[:, None] // GLA_DK == col_h[None, :]).astype(np.float32)
    pair = np.arange(_PAIR)
    pi, pj = pair // S, pair % S
    gsum = (np.arange(S)[:, None] == pi[None, :]).astype(np.float32)
    dmask = np.zeros((2, _NSUB, _PAIR, _QK_LANES), np.float32)
    for dr in range(2):
        earlier = (pj <= pi) if dr == 0 else (pj >= pi)
        for I in range(_NSUB):
            dmask[dr, I] = (earlier[:, None] & (pj[:, None] == col_jj[None, :])
                            & (col_j[None, :] == I))
    rows64 = np.arange(4 * S)
    hmask_k = (rows64[:, None] // S == lane_qk[None, :] // GLA_DK).astype(np.float32)
    lane_v = np.arange(_V_LANES)
    hmask_v = (rows64[:, None] // S == lane_v[None, :] // GLA_DV).astype(np.float32)
    bdmask = (lane_v[:, None] // GLA_DV == lane_qk[None, :] // GLA_DK).astype(np.float32)
    return (jnp.asarray(cum), jnp.asarray(seg, BF16), jnp.asarray(gsum, BF16), jnp.asarray(dmask),
            jnp.asarray(hmask_k), jnp.asarray(hmask_v), jnp.asarray(bdmask))


def _gla_kernel(*refs, L, has_state, emit_state):
    (q_ref, k_ref, v_ref, og_ref, lr_ref, up_ref, upb_ref, gn_ref, cum_ref, seg_ref, gsum_ref,
     dmask_ref, hmk_ref, hmv_ref, bdm_ref) = refs[:15]
    pos = 15
    s0_ref = None
    if has_state:
        s0_ref = refs[pos]
        pos += 1
    o_ref = refs[pos]
    pos += 1
    st_out_ref = None
    if emit_state:
        st_out_ref = refs[pos]
        pos += 1
    st_scr, oacc_scr = refs[pos], refs[pos + 1]

    nchunk = L // GLA_CHUNK
    C, S = GLA_CHUNK, GLA_SUB
    nt_dims = (((1,), (1,)), ((), ()))
    tn_dims = (((0,), (0,)), ((), ()))

    for direction in range(2):
        rev = direction == 1
        if has_state:
            st_scr[...] = s0_ref[direction]
        else:
            st_scr[...] = jnp.zeros_like(st_scr)

        def chunk_body(ci, carry, rev=rev, direction=direction):
            c = (nchunk - 1 - ci) if rev else ci
            r0 = pl.multiple_of(c * C, C)
            q = q_ref[pl.ds(r0, C), :] * (GLA_DK ** -0.5)
            k = k_ref[pl.ds(r0, C), :]
            v = v_ref[pl.ds(r0, C), :]
            x = jnp.dot(lr_ref[pl.ds(r0, C), :], up_ref[direction], precision=HIGHEST,
                        preferred_element_type=F32) + upb_ref[direction]
            g = (jnp.minimum(x, 0.0) - jnp.log1p(jnp.exp(-jnp.abs(x)))) / GLA_TAU
            b = jnp.dot(cum_ref[direction], g, precision=HIGHEST, preferred_element_type=F32)
            b_end = b[0:1] if rev else b[C - 1:C]

            st = st_scr[...]
            o = lax.dot_general((q * jnp.exp(b)).astype(BF16), st.astype(BF16), nt_dims,
                                preferred_element_type=F32)

            xs = []
            for I in range(_NSUB):
                q_i, k_i, b_i = q[I * S:(I + 1) * S], k[I * S:(I + 1) * S], b[I * S:(I + 1) * S]
                for i in range(S):
                    dec = jnp.exp(jnp.minimum(b_i[i:i + 1] - b_i, 0.0))
                    xs.append((q_i[i:i + 1] * k_i) * dec)
            xall = jnp.concatenate(xs, axis=0).astype(BF16)
            rall = jnp.dot(xall, seg_ref[...], preferred_element_type=F32)

            a_rows = []
            for I in range(_NSUB):
                rm = (rall[I * _PAIR:(I + 1) * _PAIR] * dmask_ref[direction, I]).astype(BF16)
                a_i = jnp.dot(gsum_ref[...], rm, preferred_element_type=F32)
                first = (I == _NSUB - 1) if rev else (I == 0)
                if not first:
                    ref_row = (I + 1) * S if rev else I * S - 1
                    r_i = b[ref_row:ref_row + 1]
                    qt = q[I * S:(I + 1) * S] * jnp.exp(b[I * S:(I + 1) * S] - r_i)
                    kt = k * jnp.exp(jnp.minimum(r_i - b, 0.0))
                    blocks = []
                    for J in range(_NSUB):
                        earlier = (J > I) if rev else (J < I)
                        if earlier:
                            kj = kt[J * S:(J + 1) * S]
                            blocks.append(jnp.concatenate([kj] * GLA_HEADS, axis=0) * hmk_ref[...])
                        else:
                            blocks.append(jnp.zeros((GLA_HEADS * S, _QK_LANES), F32))
                    rhs = jnp.concatenate(blocks, axis=0).astype(BF16)
                    a_i = a_i + lax.dot_general(qt.astype(BF16), rhs, nt_dims,
                                                preferred_element_type=F32)
                a_rows.append(a_i)
            a = jnp.concatenate(a_rows, axis=0).astype(BF16)

            vblocks = []
            for J in range(_NSUB):
                vj = v[J * S:(J + 1) * S]
                vblocks.append(jnp.concatenate([vj] * GLA_HEADS, axis=0) * hmv_ref[...])
            vbd = jnp.concatenate(vblocks, axis=0).astype(BF16)
            o = o + jnp.dot(a, vbd, preferred_element_type=F32)
            if rev:
                oacc_scr[pl.ds(r0, C), :] = oacc_scr[pl.ds(r0, C), :] + o
            else:
                oacc_scr[pl.ds(r0, C), :] = o

            kl = (k * jnp.exp(b_end - b)).astype(BF16)
            u = lax.dot_general(v.astype(BF16), kl, tn_dims, preferred_element_type=F32)
            st_scr[...] = st * jnp.exp(b_end) + u * bdm_ref[...]
            return carry

        lax.fori_loop(0, nchunk, chunk_body, 0)
        if emit_state:
            st_out_ref[direction] = st_scr[...]

    for h in range(GLA_HEADS):
        cs = slice(h * GLA_DV, (h + 1) * GLA_DV)
        oh = _rms(oacc_scr[:, cs], gn_ref[...])
        o_ref[:, cs] = (oh * _silu(og_ref[:, cs])).astype(BF16)


def _gla_call(z, up, upb, gn, consts, s0t, L, nseq, blk0, emit_state):
    has_state = s0t is not None
    cum, seg, gsum, dmask, hmk, hmv, bdm = consts
    full = lambda shape: pl.BlockSpec(shape, lambda s: (0,) * len(shape))
    in_specs = [
        pl.BlockSpec((L, _QK_LANES), lambda s: (blk0 + s, Z_QG // _QK_LANES)),
        pl.BlockSpec((L, _QK_LANES), lambda s: (blk0 + s, Z_KG // _QK_LANES)),
        pl.BlockSpec((L, _V_LANES), lambda s: (blk0 + s, Z_VG // _V_LANES)),
        pl.BlockSpec((L, _V_LANES), lambda s: (blk0 + s, Z_OG // _V_LANES)),
        pl.BlockSpec((L, 128), lambda s: (blk0 + s, Z_LR // 128)),
        full(up.shape), full(upb.shape), full(gn.shape), full(cum.shape), full(seg.shape),
        full(gsum.shape), full(dmask.shape), full(hmk.shape), full(hmv.shape), full(bdm.shape),
    ]
    args = [z, z, z, z, z, up, upb, gn, cum, seg, gsum, dmask, hmk, hmv, bdm]
    if has_state:
        in_specs.append(pl.BlockSpec((None, 2, _V_LANES, _QK_LANES), lambda s: (s, 0, 0, 0)))
        args.append(s0t)
    out_shape = [jax.ShapeDtypeStruct((nseq * L, GLA_WIDTH), BF16)]
    out_specs = [pl.BlockSpec((L, GLA_WIDTH), lambda s: (s, 0))]
    if emit_state:
        out_shape.append(jax.ShapeDtypeStruct((nseq, 2, _V_LANES, _QK_LANES), F32))
        out_specs.append(pl.BlockSpec((None, 2, _V_LANES, _QK_LANES), lambda s: (s, 0, 0, 0)))
    return pl.pallas_call(
        functools.partial(_gla_kernel, L=L, has_state=has_state, emit_state=emit_state),
        out_shape=out_shape,
        grid=(nseq,),
        in_specs=in_specs,
        out_specs=out_specs,
        scratch_shapes=[pltpu.VMEM((_V_LANES, _QK_LANES), F32), pltpu.VMEM((L, _V_LANES), F32)],
        compiler_params=pltpu.CompilerParams(
            dimension_semantics=("arbitrary",), vmem_limit_bytes=VMEM_LIMIT),
        name="gla_mixer_%d" % L,
    )(*args)


_TQ = 256


def _rope(x, cos, sin_signed, first_of_pair):
    partner = jnp.where(first_of_pair, pltpu.roll(x, HEAD_DIM - 32, 1), pltpu.roll(x, 32, 1))
    return x * cos + partner * sin_signed


def _attn_kernel(*refs, L, latent):
    q_ref, k_ref, v_ref, qn_ref, kn_ref = refs[:5]
    if latent:
        cos_ref, sin_ref, ck_ref, cv_ref, o_ref = refs[5:10]
    else:
        o_ref, ka_ref, va_ref = refs[5:8]
    G = ATT_HEADS // ATT_KV_HEADS
    scale = HEAD_DIM ** -0.5
    kn = _rms(k_ref[...], kn_ref[...])
    v = v_ref[...]
    if latent:
        lane = lax.broadcasted_iota(jnp.int32, (1, HEAD_DIM), 1)
        first_of_pair = (lane % 64) < 32
        kr = _rope(kn, cos_ref[...], sin_ref[...], first_of_pair)
        keys = jnp.concatenate([ck_ref[...], kr], axis=0).astype(BF16)
        vals = jnp.concatenate([cv_ref[...], v], axis=0).astype(BF16)
    else:
        ka_ref[...] = kn
        va_ref[...] = v
        keys = kn.astype(BF16)
        vals = v.astype(BF16)
    for g in range(G):
        cs = slice(g * HEAD_DIM, (g + 1) * HEAD_DIM)
        for qb in range(L // _TQ):
            rs = slice(qb * _TQ, (qb + 1) * _TQ)
            qn = _rms(q_ref[rs, cs], qn_ref[...])
            if latent:
                qn = _rope(qn, cos_ref[rs, :], sin_ref[rs, :], first_of_pair)
            s = lax.dot_general(qn.astype(BF16), keys, (((1,), (1,)), ((), ())),
                                preferred_element_type=F32) * scale
            p = jnp.exp(s - jnp.max(s, axis=-1, keepdims=True))
            p = p / jnp.sum(p, axis=-1, keepdims=True)
            o = jnp.dot(p.astype(BF16), vals, preferred_element_type=F32)
            o_ref[rs, cs] = o.astype(BF16)


def _rope_tables():
    rows = DEC_SEQ // GRID_W
    half = HEAD_DIM // 2
    quarter = half // 2
    freqs = 1.0 / (ROPE_THETA ** (jnp.arange(quarter, dtype=jnp.float32) / quarter))
    row = jnp.repeat(jnp.arange(rows), GRID_W).astype(jnp.float32)
    col = jnp.tile(jnp.arange(GRID_W), rows).astype(jnp.float32)

    def tab(p):
        ang = p[:, None] * freqs[None, :]
        c, s = jnp.cos(ang), jnp.sin(ang)
        return jnp.concatenate([c, c], axis=-1), jnp.concatenate([-s, s], axis=-1)

    cr, sr = tab(row)
    cc, sc = tab(col)
    return jnp.concatenate([cr, cc], axis=-1), jnp.concatenate([sr, sc], axis=-1)


def _attn_call(z, qn, kn, L, nseq, blk0, latent, rope=None, ck=None, cv=None, layer=0):
    G = ATT_HEADS // ATT_KV_HEADS
    qw = G * HEAD_DIM
    in_specs = [
        pl.BlockSpec((L, qw), lambda s, kv: (blk0 + s, Z_QA // qw + kv)),
        pl.BlockSpec((L, HEAD_DIM), lambda s, kv: (blk0 + s, Z_KA // HEAD_DIM + kv)),
        pl.BlockSpec((L, HEAD_DIM), lambda s, kv: (blk0 + s, Z_VA // HEAD_DIM + kv)),
        pl.BlockSpec((1, HEAD_DIM), lambda s, kv: (0, 0)),
        pl.BlockSpec((1, HEAD_DIM), lambda s, kv: (0, 0)),
    ]
    args = [z, z, z, qn, kn]
    out_shape = [jax.ShapeDtypeStruct((nseq * L, ATT_WIDTH), BF16)]
    out_specs = [pl.BlockSpec((L, qw), lambda s, kv: (s, kv))]
    if latent:
        in_specs += [
            pl.BlockSpec((L, HEAD_DIM), lambda s, kv: (0, 0)),
            pl.BlockSpec((L, HEAD_DIM), lambda s, kv: (0, 0)),
            pl.BlockSpec((None, None, PAST_LEN, HEAD_DIM), lambda s, kv: (s, layer, 0, kv)),
            pl.BlockSpec((None, None, PAST_LEN, HEAD_DIM), lambda s, kv: (s, layer, 0, kv)),
        ]
        args += [rope[0], rope[1], ck, cv]
    else:
        kv_shape = jax.ShapeDtypeStruct((nseq, L, ATT_KV_HEADS * HEAD_DIM), F32)
        out_shape += [kv_shape, kv_shape]
        kv_spec = pl.BlockSpec((None, L, HEAD_DIM), lambda s, kv: (s, 0, kv))
        out_specs += [kv_spec, kv_spec]
    return pl.pallas_call(
        functools.partial(_attn_kernel, L=L, latent=latent),
        out_shape=out_shape,
        grid=(nseq, ATT_KV_HEADS),
        in_specs=in_specs,
        out_specs=out_specs,
        compiler_params=pltpu.CompilerParams(
            dimension_semantics=("arbitrary", "arbitrary"), vmem_limit_bytes=VMEM_LIMIT),
        name="attn_mixer_%d" % L,
    )(*args)


def _route(logits):
    lane = lax.broadcasted_iota(jnp.int32, logits.shape, 1)
    lane_f = lane.astype(F32)
    neg = jnp.float32(-jnp.inf)
    big = jnp.float32(ROUTE_LANES)
    is_g = lane < N_GROUPS
    gl = jnp.where(is_g, logits, neg)
    gmax = jnp.max(gl, axis=-1, keepdims=True)
    g_idx = jnp.min(jnp.where(gl == gmax, lane_f, big), axis=-1, keepdims=True)
    g_w = 1.0 / jnp.sum(jnp.where(is_g, jnp.exp(gl - gmax), 0.0), axis=-1, keepdims=True)
    e_lane = lane - N_GROUPS
    in_group = (e_lane >= 0) & (e_lane < N_EXPERTS) & (
        jnp.right_shift(e_lane, 3).astype(F32) == g_idx)
    el = jnp.where(in_group, logits, neg)
    m1 = jnp.max(el, axis=-1, keepdims=True)
    i1 = jnp.min(jnp.where(el == m1, lane_f, big), axis=-1, keepdims=True)
    el2 = jnp.where(lane_f == i1, neg, el)
    m2 = jnp.max(el2, axis=-1, keepdims=True)
    i2 = jnp.min(jnp.where(el2 == m2, lane_f, big), axis=-1, keepdims=True)
    zsum = jnp.sum(jnp.where(in_group, jnp.exp(el - m1), 0.0), axis=-1, keepdims=True)
    p1 = 1.0 / zsum
    p2 = jnp.exp(m2 - m1) / zsum
    w1 = p1 / (p1 + p2) * g_w
    w2 = p2 / (p1 + p2) * g_w
    out = jnp.where(lane == 0, i1 - N_GROUPS, 0.0)
    out = jnp.where(lane == 1, i2 - N_GROUPS, out)
    out = jnp.where(lane == 2, w1, out)
    out = jnp.where(lane == 3, w2, out)
    return out


def _outproj_kernel(pc_ref, pl_ref, gc_ref, gl_ref, ac_ref, al_ref, xc_ref, xl_ref, mod_ref, wo_ref,
                    nf_ref, wr_ref, br_ref, x1_ref, h2_ref, route_ref):
    mixed = jnp.dot(_pick(pc_ref, pl_ref), wo_ref[0:POOL_WIDTH, :], preferred_element_type=F32)
    mixed = mixed + jnp.dot(_pick(gc_ref, gl_ref), wo_ref[POOL_WIDTH:POOL_WIDTH + GLA_WIDTH, :],
                            preferred_element_type=F32)
    mixed = mixed + jnp.dot(_pick(ac_ref, al_ref), wo_ref[POOL_WIDTH + GLA_WIDTH:, :],
                            preferred_element_type=F32)
    gate1 = mod_ref[:, 2 * D_MODEL:3 * D_MODEL]
    shift2 = mod_ref[:, 3 * D_MODEL:4 * D_MODEL]
    scale2 = mod_ref[:, 4 * D_MODEL:5 * D_MODEL]
    x1 = _pick(xc_ref, xl_ref) + gate1 * mixed
    x1_ref[...] = x1
    h2 = _rms(x1, nf_ref[...]) * (1.0 + scale2) + shift2
    h2_ref[...] = h2
    h_hi = h2.astype(BF16)
    h_lo = (h2 - h_hi.astype(F32)).astype(BF16)
    l2 = jnp.dot(h_hi, wr_ref[...], preferred_element_type=F32)
    logits = (l2[:, :ROUTE_LANES] + l2[:, ROUTE_LANES:] + br_ref[...]
              + jnp.dot(h_lo, wr_ref[:, :ROUTE_LANES], preferred_element_type=F32))
    route_ref[...] = _route(logits)


def _outproj_call(pool_o, gla_o, attn_o, x, mod3, wo, nf, wr, br):
    tile = lambda w: pl.BlockSpec((TM, w), lambda i: (i, 0))
    ctx = lambda w: pl.BlockSpec((TM, w), lambda i: (_ctx_tile(i), 0))
    lat = lambda w: pl.BlockSpec((TM, w), lambda i: (_lat_tile(i), 0))
    return pl.pallas_call(
        _outproj_kernel,
        out_shape=[jax.ShapeDtypeStruct((T_ALL, D_MODEL), F32),
                   jax.ShapeDtypeStruct((T_ALL, D_MODEL), F32),
                   jax.ShapeDtypeStruct((T_ALL, ROUTE_LANES), F32)],
        grid=(N_TOK_TILES,),
        in_specs=[
            ctx(POOL_WIDTH), lat(POOL_WIDTH), ctx(GLA_WIDTH), lat(GLA_WIDTH),
            ctx(ATT_WIDTH), lat(ATT_WIDTH), ctx(D_MODEL), lat(D_MODEL),
            pl.BlockSpec((None, 1, 6 * D_MODEL), lambda i: (_mod_row(i), 0, 0)),
            pl.BlockSpec((D_MODEL, D_MODEL), lambda i: (0, 0)),
            pl.BlockSpec((1, D_MODEL), lambda i: (0, 0)),
            pl.BlockSpec((D_MODEL, 2 * ROUTE_LANES), lambda i: (0, 0)),
            pl.BlockSpec((1, ROUTE_LANES), lambda i: (0, 0)),
        ],
        out_specs=[tile(D_MODEL), tile(D_MODEL), tile(ROUTE_LANES)],
        compiler_params=pltpu.CompilerParams(
            dimension_semantics=("arbitrary",), vmem_limit_bytes=VMEM_LIMIT),
        name="out_proj_router",
    )(*pool_o, *gla_o, *attn_o, *x, mod3, wo, nf, wr, br)


def _row_copy(src_ref, src_row, dst_ref, dst_row, sem):
    return pltpu.make_async_copy(src_ref.at[pl.ds(src_row, 1)], dst_ref.at[pl.ds(dst_row, 1)], sem)


_CAST_ROWS = 256
_MOE_STEPS = MOE_TILES + 1


def _expert_kernel(pos_ref, te_ref, nreal_ref, nused_ref, h_ref, wi_ref, wo_ref, y2_ref,
                   inv_scr, xbuf, ybuf, wi_bf, wo_bf, gsem, ssem):
    i = pl.program_id(0)
    n_used = nused_ref[0]

    def gather(tile, slot, start):
        def body(r, carry):
            if start:
                s = inv_scr[tile * MOE_TILE + r]
                _row_copy(h_ref, jnp.bitwise_and(s, T_ALL - 1), xbuf.at[slot], r,
                          gsem.at[slot]).start(priority=1)
            else:
                _row_copy(h_ref, 0, xbuf.at[slot], 0, gsem.at[slot]).wait()
            return carry
        lax.fori_loop(0, nreal_ref[tile], body, 0)

    def scatter(tile, slot, start):
        def body(r, carry):
            if start:
                s = inv_scr[tile * MOE_TILE + r]
                _row_copy(ybuf.at[slot], r, y2_ref, s, ssem.at[slot]).start(priority=1)
            else:
                _row_copy(ybuf.at[slot], 0, y2_ref, 0, ssem.at[slot]).wait()
            return carry
        lax.fori_loop(0, nreal_ref[tile], body, 0)

    @pl.when(i == 0)
    def _():
        def invert(s, carry):
            inv_scr[pos_ref[s]] = s
            return carry
        lax.fori_loop(0, MOE_SLOTS, invert, 0, unroll=8)
        xbuf[...] = jnp.zeros_like(xbuf)
        gather(0, 0, True)

    slot = i % 2

    @pl.when(i < n_used)
    def _():
        gather(i, slot, False)

        @pl.when(i + 1 < n_used)
        def _():
            gather(i + 1, 1 - slot, True)

        @pl.when(i >= 2)
        def _():
            scatter(i - 2, slot, False)

        e = te_ref[i]
        prev = te_ref[jnp.maximum(i - 1, 0)]

        @pl.when((i == 0) | (e != prev))
        def _():
            for r in range(0, D_MODEL, _CAST_ROWS):
                wi_bf[r:r + _CAST_ROWS, :] = wi_ref[r:r + _CAST_ROWS, :].astype(BF16)
            for r in range(0, D_EXPERT, _CAST_ROWS):
                wo_bf[r:r + _CAST_ROWS, :] = wo_ref[r:r + _CAST_ROWS, :].astype(BF16)

        hid = jnp.dot(xbuf[slot].astype(BF16), wi_bf[...], preferred_element_type=F32)
        act = _silu(hid[:, :D_EXPERT]) * hid[:, D_EXPERT:]
        ybuf[slot] = jnp.dot(act.astype(BF16), wo_bf[...], preferred_element_type=F32)
        scatter(i, slot, True)

    @pl.when(i == n_used)
    def _():
        scatter(i - 1, 1 - slot, False)

        @pl.when(i >= 2)
        def _():
            scatter(i - 2, slot, False)


def _expert_call(plan, h2, w_ei, w_eo, layer):
    pos, tile_expert, nreal, n_used = plan
    return pl.pallas_call(
        _expert_kernel,
        out_shape=jax.ShapeDtypeStruct((MOE_SLOTS, D_MODEL), F32),
        grid_spec=pltpu.PrefetchScalarGridSpec(
            num_scalar_prefetch=4,
            grid=(_MOE_STEPS,),
            in_specs=[
                pl.BlockSpec(memory_space=pl.ANY),
                pl.BlockSpec((None, None, D_MODEL, 2 * D_EXPERT),
                             lambda i, pos, te, nr, nu: (layer, te[i], 0, 0)),
                pl.BlockSpec((None, None, D_EXPERT, D_MODEL),
                             lambda i, pos, te, nr, nu: (layer, te[i], 0, 0)),
            ],
            out_specs=pl.BlockSpec(memory_space=pl.ANY),
            scratch_shapes=[
                pltpu.SMEM((_MOE_STEPS * MOE_TILE,), jnp.int32),
                pltpu.VMEM((2, MOE_TILE, D_MODEL), F32),
                pltpu.VMEM((2, MOE_TILE, D_MODEL), F32),
                pltpu.VMEM((D_MODEL, 2 * D_EXPERT), BF16),
                pltpu.VMEM((D_EXPERT, D_MODEL), BF16),
                pltpu.SemaphoreType.DMA((2,)),
                pltpu.SemaphoreType.DMA((2,)),
            ],
        ),
        compiler_params=pltpu.CompilerParams(
            dimension_semantics=("arbitrary",), vmem_limit_bytes=VMEM_LIMIT),
        name="moe_experts",
    )(pos, tile_expert, nreal, n_used, h2, w_ei, w_eo)


def _combine_kernel(x1_ref, route_ref, mod_ref, ya_ref, yb_ref, oc_ref, ol_ref):
    gate2 = mod_ref[:, 5 * D_MODEL:6 * D_MODEL]
    w1 = route_ref[:, 2:3]
    w2 = route_ref[:, 3:4]
    y = w1 * ya_ref[...] + w2 * yb_ref[...]
    out = x1_ref[...] + gate2 * y
    is_ctx = pl.program_id(0) < CTX_TILES

    @pl.when(is_ctx)
    def _():
        oc_ref[...] = out

    @pl.when(jnp.logical_not(is_ctx))
    def _():
        ol_ref[...] = out


def _combine_call(x1, route, mod3, y2):
    return pl.pallas_call(
        _combine_kernel,
        out_shape=[jax.ShapeDtypeStruct((T_CTX, D_MODEL), F32),
                   jax.ShapeDtypeStruct((T_LAT, D_MODEL), F32)],
        grid=(N_TOK_TILES,),
        in_specs=[
            pl.BlockSpec((TM, D_MODEL), lambda i: (i, 0)),
            pl.BlockSpec((TM, ROUTE_LANES), lambda i: (i, 0)),
            pl.BlockSpec((None, 1, 6 * D_MODEL), lambda i: (_mod_row(i), 0, 0)),
            pl.BlockSpec((TM, D_MODEL), lambda i: (i, 0)),
            pl.BlockSpec((TM, D_MODEL), lambda i: (i + N_TOK_TILES, 0)),
        ],
        out_specs=[pl.BlockSpec((TM, D_MODEL), lambda i: (_ctx_tile(i), 0)),
                   pl.BlockSpec((TM, D_MODEL), lambda i: (_lat_tile(i), 0))],
        compiler_params=pltpu.CompilerParams(
            dimension_semantics=("arbitrary",), vmem_limit_bytes=VMEM_LIMIT),
        name="moe_combine",
    )(x1, route, mod3, y2, y2)


def _dispatch_plan(route):
    ef = route[:, 0:2].astype(jnp.int32).T.reshape(-1)
    onehot = (ef[:, None] == jnp.arange(N_EXPERTS, dtype=jnp.int32)[None, :]).astype(jnp.int32)
    csum = jnp.cumsum(onehot, axis=0)
    rank = jnp.sum(csum * onehot, axis=1) - 1
    counts = csum[-1]
    ntiles = (counts + MOE_TILE - 1) // MOE_TILE
    tile_end = jnp.cumsum(ntiles)
    tile_start = tile_end - ntiles
    pos = jnp.sum(onehot * tile_start[None, :], axis=1) * MOE_TILE + rank
    n_used = tile_end[-1]
    ti = jnp.arange(_MOE_STEPS, dtype=jnp.int32)
    valid = ti < n_used
    tc = jnp.where(valid, ti, n_used - 1)
    tile_expert = jnp.minimum(jnp.sum((tc[:, None] >= tile_end[None, :]).astype(jnp.int32), axis=1),
                              N_EXPERTS - 1)
    sel = (tile_expert[:, None] == jnp.arange(N_EXPERTS, dtype=jnp.int32)[None, :]).astype(jnp.int32)
    start_of = jnp.sum(sel * tile_start[None, :], axis=1)
    cnt_of = jnp.sum(sel * counts[None, :], axis=1)
    nreal = jnp.where(valid, jnp.clip(cnt_of - (tc - start_of) * MOE_TILE, 0, MOE_TILE), 0)
    return (pos.astype(jnp.int32), tile_expert.astype(jnp.int32), nreal.astype(jnp.int32),
            n_used.astype(jnp.int32).reshape(1))


def _reorder_w_in(w):
    lr0 = 512 + 256 + 256 + 512 + 512
    parts = [w[:, :lr0], w[:, lr0 + 2 * GLA_GATE_RANK:], w[:, lr0:lr0 + 2 * GLA_GATE_RANK],
             jnp.zeros((D_MODEL, Z_COLS - Z_LR - 2 * GLA_GATE_RANK), w.dtype)]
    return jnp.concatenate(parts, axis=1).astype(BF16)


def _state_to_blockdiag_t(s):
    st = jnp.swapaxes(s, -1, -2)
    eye = jnp.eye(GLA_HEADS, dtype=s.dtype)
    full = st[:, :, :, :, None, :] * eye[None, None, :, None, :, None]
    return full.reshape(s.shape[0], 2, _V_LANES, _QK_LANES)


def _blockdiag_t_to_state(st):
    b = st.shape[0]
    full = st.reshape(b, 2, GLA_HEADS, GLA_DV, GLA_HEADS, GLA_DK)
    diag = jnp.stack([full[:, :, h, :, h, :] for h in range(GLA_HEADS)], axis=2)
    return jnp.swapaxes(diag, -1, -2)


def kernel(x_prompt, x_sample, cache_k, cache_v, state_gla, c, c_ctx, w_mod, b_mod, norm_mix, w_in,
           w_pool, pool_scale, gla_up, gla_up_b, gla_norm, q_norm, k_norm, w_out, norm_ffn,
           w_group_router, b_group_router, w_expert_router, b_expert_router, w_expert_in,
           w_expert_out):
    x = (x_prompt.reshape(T_CTX, D_MODEL), x_sample.reshape(T_LAT, D_MODEL))
    c8 =jnp.concatenate([c, c_ctx[None, :], jnp.zeros((8 - DEC_BATCH - 1, D_MODEL), F32)], axis=0)
    mod = _mod_call(c8, w_mod, b_mod)
    rope = _rope_tables()
    gla_consts = _gla_consts()
    ck = cache_k.reshape(DEC_BATCH, DEPTH, PAST_LEN, ATT_KV_HEADS * HEAD_DIM)
    cv = cache_v.reshape(DEC_BATCH, DEPTH, PAST_LEN, ATT_KV_HEADS * HEAD_DIM)
    lat_blk0 = T_CTX // DEC_SEQ

    new_k, new_v, new_s = [], [], []
    for l in range(DEPTH):
        mod3 = mod[l].reshape(8, 1, 6 * D_MODEL)
        z = _inproj_call(x[0], x[1], mod3, norm_mix[l][None, :], _reorder_w_in(w_in[l]))

        wp = w_pool[l].astype(BF16)
        ps = pool_scale[l][None, :]
        pool_o = (_pool_call(z, wp, ps, SEQ, BATCH, 0),
                  _pool_call(z, wp, ps, DEC_SEQ, DEC_BATCH, lat_blk0))

        up = jnp.zeros((2, 128, _QK_LANES), F32)
        up = up.at[0, 0:GLA_GATE_RANK].set(gla_up[l, 0])
        up = up.at[1, GLA_GATE_RANK:2 * GLA_GATE_RANK].set(gla_up[l, 1])
        upb = gla_up_b[l][:, None, :]
        gn = gla_norm[l][None, :]
        gla_c, st_c = _gla_call(z, up, upb, gn, gla_consts, None, SEQ, BATCH, 0, True)
        (gla_l,) = _gla_call(z, up, upb, gn, gla_consts, _state_to_blockdiag_t(state_gla[:, l]),
                             DEC_SEQ, DEC_BATCH, lat_blk0, False)
        gla_o = (gla_c, gla_l)
        new_s.append(_blockdiag_t_to_state(st_c))

        qn = q_norm[l][None, :]
        kn = k_norm[l][None, :]
        attn_c, ka, va = _attn_call(z, qn, kn, SEQ, BATCH, 0, False)
        (attn_l,) = _attn_call(z, qn, kn, DEC_SEQ, DEC_BATCH, lat_blk0, True, rope, ck, cv, l)
        attn_o = (attn_c, attn_l)
        new_k.append(ka.reshape(BATCH, SEQ, ATT_KV_HEADS, HEAD_DIM))
        new_v.append(va.reshape(BATCH, SEQ, ATT_KV_HEADS, HEAD_DIM))

        wr = jnp.concatenate([w_group_router[l], w_expert_router[l],
                              jnp.zeros((D_MODEL, ROUTE_LANES - N_GROUPS - N_EXPERTS), F32)], axis=1)
        wr_hi = wr.astype(BF16)
        wr = jnp.concatenate([wr_hi, (wr - wr_hi.astype(F32)).astype(BF16)], axis=1)
        br = jnp.concatenate([b_group_router[l], b_expert_router[l],
                              jnp.zeros((ROUTE_LANES - N_GROUPS - N_EXPERTS,), F32)])[None, :]
        x1, h2, route = _outproj_call(pool_o, gla_o, attn_o, x, mod3, w_out[l].astype(BF16),
                                      norm_ffn[l][None, :], wr, br)

        y2 = _expert_call(_dispatch_plan(route), h2, w_expert_in, w_expert_out, l)
        x = _combine_call(x1, route, mod3, y2)

    y_prompt = x[0].reshape(BATCH, SEQ, D_MODEL)
    y_sample = x[1].reshape(DEC_BATCH, DEC_SEQ, D_MODEL)
    return (y_prompt, y_sample, jnp.stack(new_k, axis=1), jnp.stack(new_v, axis=1),
            jnp.stack(new_s, axis=1))
```
